```python
import jax, jax.numpy as jnp
from jax import lax
import numpy as np

D_MODEL = 1024
BATCH = 8
SEQ = 2048
DEPTH = 4

N_A_LAYERS = DEPTH // 2
N_B_LAYERS = DEPTH - N_A_LAYERS
GDN_HEADS = D_MODEL // 128
GDN_DK = 128
GDN_DV = 128
GDN_CONV = 4
GDN_CHUNK = 64
SWA_HEAD_DIM = 64
SWA_Q_HEADS = D_MODEL // SWA_HEAD_DIM
SWA_KV_HEADS = 4
SWA_WINDOW = 128
SWA_BLOCK = 128
D_FF = 4 * D_MODEL
NORM_EPS = 1e-6
MOD_STD = 0.1

GDN_KW = GDN_HEADS * GDN_DK
GDN_VW = GDN_HEADS * GDN_DV
GDN_IN = 2 * GDN_KW + 2 * GDN_VW + 2 * GDN_HEADS
SWA_QW = SWA_Q_HEADS * SWA_HEAD_DIM
SWA_KVW = SWA_KV_HEADS * SWA_HEAD_DIM

kernel_name = "yoco_gdn_swa_sink_alibi_sandwich_adaln"


def rms_norm(x, gain):
    xf = x.astype(jnp.float32)
    y = xf * lax.rsqrt(jnp.mean(xf * xf, axis=-1, keepdims=True) + NORM_EPS)
    return (y * gain.astype(jnp.float32)).astype(x.dtype)


def l2norm(x):
    xf = x.astype(jnp.float32)
    return (xf * lax.rsqrt(jnp.sum(xf * xf, axis=-1, keepdims=True) + NORM_EPS)).astype(x.dtype)


def modulation(c, w, b):
    return jax.nn.silu(c) @ w + b


def adaln_pre(x, gain, shift, scale):
    return rms_norm(x, gain) * (1.0 + scale[:, None, :]) + shift[:, None, :]


def adaln_post(x, y, gain, gate):
    return x + (1.0 + gate[:, None, :]) * rms_norm(y, gain)


def causal_depthwise_conv(x, w):
    K, C = w.shape
    return lax.conv_general_dilated(x, w[:, None, :].astype(x.dtype), window_strides=(1,),
                                    padding=((K - 1, 0),),
                                    dimension_numbers=('NWC', 'WIO', 'NWC'),
                                    feature_group_count=C)


def gated_delta_rule(q, k, v, g, beta):
    out_dtype = v.dtype
    Bn, Sn, H, dk = q.shape
    dv = v.shape[-1]
    C = GDN_CHUNK
    N = Sn // C
    f32 = jnp.float32

    def chunk(t):
        t = t.astype(f32).reshape((Bn, N, C, H) + t.shape[3:])
        return jnp.moveaxis(t, 3, 1)

    q = chunk(q) * (dk ** -0.5)
    k = chunk(k)
    v = chunk(v)
    beta = chunk(beta)
    g = jnp.cumsum(chunk(g), axis=-1)
    idx = jnp.arange(C)
    incl = idx[:, None] >= idx[None, :]
    strict = idx[:, None] > idx[None, :]
    decay = jnp.exp(jnp.where(incl, g[..., :, None] - g[..., None, :], -jnp.inf))
    k_beta = k * beta[..., None]
    v_beta = v * beta[..., None]
    L = jnp.where(strict, jnp.einsum('bhncd,bhnsd->bhncs', k_beta, k) * decay, 0.0)
    eye = jnp.eye(C, dtype=f32)
    T = lax.linalg.triangular_solve(eye + L, jnp.broadcast_to(eye, L.shape), left_side=True,
                                    lower=True, unit_diagonal=True)
    u = jnp.einsum('bhncs,bhnse->bhnce', T, v_beta)
    w = jnp.einsum('bhncs,bhnsd->bhncd', T, k_beta * jnp.exp(g)[..., None])
    attn_intra = jnp.einsum('bhncd,bhnsd->bhncs', q, k) * decay
    q_g = q * jnp.exp(g)[..., None]
    k_g = k * jnp.exp(g[..., -1:] - g)[..., None]
    g_last = jnp.exp(g[..., -1])

    def step(S, inp):
        u_n, w_n, qg_n, kg_n, a_n, gl_n = inp
        v_new = u_n - jnp.einsum('bhcd,bhde->bhce', w_n, S)
        o = jnp.einsum('bhcd,bhde->bhce', qg_n, S) + jnp.einsum('bhcs,bhse->bhce', a_n, v_new)
        S = S * gl_n[..., None, None] + jnp.einsum('bhcd,bhce->bhde', kg_n, v_new)
        return S, o

    xs = tuple(jnp.moveaxis(t, 2, 0) for t in (u, w, q_g, k_g, attn_intra, g_last))
    S0 = jnp.zeros((Bn, H, dk, dv), f32)
    _, o = lax.scan(step, S0, xs)
    o = jnp.moveaxis(jnp.moveaxis(o, 0, 2), 1, 3)
    return o.reshape(Bn, Sn, H, dv).astype(out_dtype)


def gdn_mixer(h, w_in, conv_w, a_log, dt_bias, onorm, w_out):
    Bn, Sn, _ = h.shape
    proj = h @ w_in
    qkv, z, a, b = jnp.split(proj, [2 * GDN_KW + GDN_VW, 2 * GDN_KW + 2 * GDN_VW,
                                    2 * GDN_KW + 2 * GDN_VW + GDN_HEADS], axis=-1)
    qkv = jax.nn.silu(causal_depthwise_conv(qkv, conv_w))
    q, k, v = jnp.split(qkv, [GDN_KW, 2 * GDN_KW], axis=-1)
    q = l2norm(q.reshape(Bn, Sn, GDN_HEADS, GDN_DK))
    k = l2norm(k.reshape(Bn, Sn, GDN_HEADS, GDN_DK))
    v = v.reshape(Bn, Sn, GDN_HEADS, GDN_DV)
    g = -jnp.exp(a_log.astype(jnp.float32)) * jax.nn.softplus(a.astype(jnp.float32) + dt_bias.astype(jnp.float32))
    beta = jax.nn.sigmoid(b.astype(jnp.float32))
    o = gated_delta_rule(q, k, v, g, beta)
    o = rms_norm(o, onorm) * jax.nn.silu(z.reshape(Bn, Sn, GDN_HEADS, GDN_DV))
    return o.reshape(Bn, Sn, GDN_VW) @ w_out


def alibi_slopes(n):
    return jnp.exp2(-8.0 * jnp.arange(1, n + 1, dtype=jnp.float32) / n)


def swa_sink_attention(q, k, v, sinks):
    Bn, Sn, Hq, hd = q.shape
    Hkv = k.shape[2]
    G = Hq // Hkv
    W = SWA_BLOCK
    N = Sn // W
    qb = q.reshape(Bn, N, W, Hkv, G, hd)

    def band(t):
        tb = t.reshape(Bn, N, W, Hkv, hd)
        prev = jnp.pad(tb, ((0, 0), (1, 0), (0, 0), (0, 0), (0, 0)))[:, :-1]
        return jnp.concatenate([prev, tb], axis=2)

    kb = band(k)
    vb = band(v)
    scores = jnp.einsum('bnqhgd,bnkhd->bnhgqk', qb, kb).astype(jnp.float32) * (hd ** -0.5)
    dist = (jnp.arange(W)[:, None] + W) - jnp.arange(2 * W)[None, :]
    valid = (dist >= 0) & (dist < SWA_WINDOW)
    blk = jnp.arange(N)
    valid = valid[None] & ((blk[:, None, None] > 0) | (jnp.arange(2 * W) >= W)[None, None, :])
    slopes = alibi_slopes(Hq).reshape(Hkv, G)
    scores = scores - slopes[:, :, None, None] * dist.astype(jnp.float32)
    scores = jnp.where(valid[None, :, None, None], scores, -jnp.inf)
    sink = sinks.astype(jnp.float32).reshape(Hkv, G)[:, :, None, None]
    m = jnp.maximum(jnp.max(scores, axis=-1, keepdims=True), sink)
    p = jnp.exp(scores - m)
    p = p / (jnp.sum(p, axis=-1, keepdims=True) + jnp.exp(sink - m))
    o = jnp.einsum('bnhgqk,bnkhd->bnqhgd', p.astype(v.dtype), vb)
    return o.reshape(Bn, Sn, Hq, hd)


def swa_mixer(h, w_q, k_sh, v_sh, sinks, w_o):
    Bn, Sn, _ = h.shape
    q = (h @ w_q).reshape(Bn, Sn, SWA_Q_HEADS, SWA_HEAD_DIM)
    o = swa_sink_attention(q, k_sh, v_sh, sinks)
    return o.reshape(Bn, Sn, SWA_QW) @ w_o


def sq_relu_mlp(h, w1, w2):
    return jnp.square(jax.nn.relu(h @ w1)) @ w2


def shared_kv(x, c, kv_mod_w, kv_mod_b, kv_norm, w_kv):
    Bn, Sn, _ = x.shape
    shift, scale = jnp.split(modulation(c, kv_mod_w, kv_mod_b), 2, axis=-1)
    h = adaln_pre(x, kv_norm, shift, scale)
    k, v = jnp.split(h @ w_kv, 2, axis=-1)
    return (k.reshape(Bn, Sn, SWA_KV_HEADS, SWA_HEAD_DIM),
            v.reshape(Bn, Sn, SWA_KV_HEADS, SWA_HEAD_DIM))


def setup_inputs(seed: int = 0) -> dict:
    key = jax.random.key(seed)
    ks = jax.random.split(key, 24)
    f32 = jnp.float32
    D = D_MODEL

    def nrm(k, shape, std):
        return jax.random.normal(k, shape, f32) * std

    dt = jnp.exp(jax.random.uniform(ks[6], (N_A_LAYERS, GDN_HEADS), f32, np.log(1e-3), np.log(1e-1)))
    return {
        "x": nrm(ks[0], (BATCH, SEQ, D), 1.0),
        "c": nrm(ks[1], (BATCH, D), 1.0),
        "mod_w": nrm(ks[2], (DEPTH, D, 6 * D), MOD_STD * D ** -0.5),
        "mod_b": nrm(ks[3], (DEPTH, 6 * D), 0.02),
        "norm_g": 1.0 + nrm(ks[4], (DEPTH, 4, D), 0.05),
        "gdn_w_in": nrm(ks[5], (N_A_LAYERS, D, GDN_IN), D ** -0.5),
        "gdn_conv": nrm(ks[7], (N_A_LAYERS, GDN_CONV, 2 * GDN_KW + GDN_VW), GDN_CONV ** -0.5),
        "gdn_a_log": jnp.log(jax.random.uniform(ks[8], (N_A_LAYERS, GDN_HEADS), f32, 1.0, 16.0)),
        "gdn_dt_bias": dt + jnp.log(-jnp.expm1(-dt)),
        "gdn_onorm": 1.0 + nrm(ks[9], (N_A_LAYERS, GDN_DV), 0.05),
        "gdn_w_out": nrm(ks[10], (N_A_LAYERS, GDN_VW, D), GDN_VW ** -0.5),
        "kv_mod_w": nrm(ks[11], (D, 2 * D), MOD_STD * D ** -0.5),
        "kv_mod_b": nrm(ks[12], (2 * D,), 0.02),
        "kv_norm": 1.0 + nrm(ks[13], (D,), 0.05),
        "w_kv": nrm(ks[14], (D, 2 * SWA_KVW), D ** -0.5),
        "swa_w_q": nrm(ks[15], (N_B_LAYERS, D, SWA_QW), D ** -0.5),
        "swa_sinks": nrm(ks[16], (N_B_LAYERS, SWA_Q_HEADS), 0.5),
        "swa_w_o": nrm(ks[17], (N_B_LAYERS, SWA_QW, D), SWA_QW ** -0.5),
        "mlp_w1": nrm(ks[18], (DEPTH, D, D_FF), D ** -0.5),
        "mlp_w2": nrm(ks[19], (DEPTH, D_FF, D), D_FF ** -0.5),
    }


def reference(x, c, mod_w, mod_b, norm_g, gdn_w_in, gdn_conv, gdn_a_log, gdn_dt_bias, gdn_onorm,
              gdn_w_out, kv_mod_w, kv_mod_b, kv_norm, w_kv, swa_w_q, swa_sinks, swa_w_o,
              mlp_w1, mlp_w2):
    k_sh = None
    v_sh = None
    for layer in range(DEPTH):
        s_mix, sc_mix, g_mix, s_mlp, sc_mlp, g_mlp = jnp.split(
            modulation(c, mod_w[layer], mod_b[layer]), 6, axis=-1)
        h = adaln_pre(x, norm_g[layer, 0], s_mix, sc_mix)
        if layer < N_A_LAYERS:
            i = layer
            y = gdn_mixer(h, gdn_w_in[i], gdn_conv[i], gdn_a_log[i], gdn_dt_bias[i],
                          gdn_onorm[i], gdn_w_out[i])
        else:
            j = layer - N_A_LAYERS
            y = swa_mixer(h, swa_w_q[j], k_sh, v_sh, swa_sinks[j], swa_w_o[j])
        x = adaln_post(x, y, norm_g[layer, 1], g_mix)
        h = adaln_pre(x, norm_g[layer, 2], s_mlp, sc_mlp)
        x = adaln_post(x, sq_relu_mlp(h, mlp_w1[layer], mlp_w2[layer]), norm_g[layer, 3], g_mlp)
        if layer == N_A_LAYERS - 1:
            k_sh, v_sh = shared_kv(x, c, kv_mod_w, kv_mod_b, kv_norm, w_kv)
    return x
```

```python
import functools
import math

import jax
import jax.numpy as jnp
from jax import lax
from jax.experimental import pallas as pl
from jax.experimental.pallas import tpu as pltpu

F32 = jnp.float32
BF16 = jnp.bfloat16

NORM_EPS = 1e-6
LANES = 128
GDN_HEADS = 8
GDN_DK = 128
GDN_CONV = 4
SWA_HEAD_DIM = 64
SWA_Q_HEADS = 16
SWA_KV_HEADS = 4
SWA_WINDOW = 128
V7X_SCOPED_VMEM_BYTES = 60000 * 1024

GDN_CHUNK = 64
CONV_HALO = 8


def _cparams(semantics, vmem_bytes):
    return pltpu.CompilerParams(dimension_semantics=semantics,
                                vmem_limit_bytes=min(int(vmem_bytes), V7X_SCOPED_VMEM_BYTES))


def _sigmoid(x):
    return 1.0 / (1.0 + jnp.exp(-x))


def _silu(x):
    return x * _sigmoid(x)


def _rms(x):
    return x * lax.rsqrt(jnp.mean(x * x, axis=-1, keepdims=True) + NORM_EPS)


def _bdot(a, b):
    return jnp.dot(a.astype(BF16), b.astype(BF16), preferred_element_type=F32)


def _bdot_nt(a, b):
    return lax.dot_general(a.astype(BF16), b.astype(BF16), (((1,), (1,)), ((), ())),
                           preferred_element_type=F32)


def _resident(shape):
    nd = len(shape)
    return pl.BlockSpec(shape, lambda *_: (0,) * nd, pipeline_mode=pl.Buffered(1))


def _mod_kernel(c_ref, w_ref, b_ref, o_ref):
    o_ref[0] = _bdot(_silu(c_ref[...]), w_ref[0]) + b_ref[0]


def modulation(c, w, b):
    L, D, N = w.shape
    B = c.shape[0]
    tn = min(N, 1024)
    assert N % tn == 0
    return pl.pallas_call(
        _mod_kernel,
        grid=(L, N // tn),
        in_specs=[pl.BlockSpec((B, D), lambda l, n: (0, 0)),
                  pl.BlockSpec((1, D, tn), lambda l, n: (l, 0, n)),
                  pl.BlockSpec((1, 1, tn), lambda l, n: (l, 0, n))],
        out_specs=pl.BlockSpec((1, B, tn), lambda l, n: (l, 0, n)),
        out_shape=jax.ShapeDtypeStruct((L, B, N), F32),
        compiler_params=_cparams(("arbitrary", "arbitrary"), 4 * D * tn * 4 + (4 << 20)),
        name="modulation",
    )(c, w, b.reshape(L, 1, N))


def _mod_spec(D, layer, k):
    return pl.BlockSpec((None, None, 1, D), lambda b, *_: (layer, b, 0, k))


def _adaln_pre(x, gain, shift, scale):
    return _rms(x) * gain * (1.0 + scale) + shift


def _gdn_gates(ab, alog_ref, dtb_ref, out_gb_ref, out_gbt_ref, chunk):
    tm = ab.shape[0]
    lane = lax.broadcasted_iota(jnp.int32, ab.shape, 1)
    pre = ab + dtb_ref[...]
    softplus = jnp.maximum(pre, 0.0) + jnp.log(1.0 + jnp.exp(-jnp.abs(pre)))
    g = -jnp.exp(alog_ref[...]) * softplus
    g = jnp.where(lane < GDN_HEADS, g, 0.0)
    beta = _sigmoid(ab)
    r = lax.broadcasted_iota(jnp.int32, (chunk, chunk), 0)
    c = lax.broadcasted_iota(jnp.int32, (chunk, chunk), 1)
    tri = (r >= c).astype(F32)
    parts = []
    for i in range(tm // chunk):
        parts.append(jnp.dot(tri, g[i * chunk:(i + 1) * chunk], preferred_element_type=F32,
                             precision=lax.Precision.HIGHEST))
    gcum = jnp.concatenate(parts, axis=0) if len(parts) > 1 else parts[0]
    gb = jnp.where(lane < GDN_HEADS, gcum, jnp.where(lane < 2 * GDN_HEADS, beta, 0.0))
    out_gb_ref[...] = gb
    out_gbt_ref[...] = gb.T[:2 * GDN_HEADS]


def _gdn_in_body(x, gain_ref, shift_ref, scale_ref, w_ref, wab_ref, alog_ref, dtb_ref,
                 p_ref, gb_ref, gbt_ref, chunk):
    h = _adaln_pre(x, gain_ref[...], shift_ref[...], scale_ref[...]).astype(BF16)
    n_slabs = w_ref.shape[1] // LANES
    group = 8
    for j0 in range(0, n_slabs, group):
        res = jnp.dot(h, w_ref[:, j0 * LANES:(j0 + group) * LANES], preferred_element_type=F32)
        for j in range(group):
            p_ref[j0 + j] = res[:, j * LANES:(j + 1) * LANES].astype(BF16)
    ab = jnp.dot(h, wab_ref[...], preferred_element_type=F32)
    _gdn_gates(ab, alog_ref, dtb_ref, gb_ref, gbt_ref, chunk)


def _gdn_in_kernel(x_ref, gain_ref, shift_ref, scale_ref, w_ref, wab_ref, alog_ref, dtb_ref,
                   p_ref, gb_ref, gbt_ref, *, chunk):
    _gdn_in_body(x_ref[...], gain_ref, shift_ref, scale_ref, w_ref, wab_ref, alog_ref, dtb_ref,
                 p_ref, gb_ref, gbt_ref, chunk)


def _gdn_in_outputs(B, S, tm, n_slabs):
    out_specs = [pl.BlockSpec((None, n_slabs, tm, LANES), lambda b, s: (b, 0, s, 0)),
                 pl.BlockSpec((None, tm, LANES), lambda b, s: (b, s, 0)),
                 pl.BlockSpec((None, 2 * GDN_HEADS, tm), lambda b, s: (b, 0, s))]
    out_shape = [jax.ShapeDtypeStruct((B, n_slabs, S, LANES), BF16),
                 jax.ShapeDtypeStruct((B, S, LANES), F32),
                 jax.ShapeDtypeStruct((B, 2 * GDN_HEADS, S), F32)]
    return out_specs, out_shape


def _gdn_weights(w_in, a_log, dt_bias):
    D, n_in = w_in.shape
    main = n_in - 2 * GDN_HEADS
    w_main = w_in[:, :main].astype(BF16)
    w_ab = jnp.zeros((D, LANES), F32).at[:, :2 * GDN_HEADS].set(w_in[:, main:]).astype(BF16)
    alog = jnp.zeros((1, LANES), F32).at[0, :GDN_HEADS].set(a_log.astype(F32))
    dtb = jnp.zeros((1, LANES), F32).at[0, :GDN_HEADS].set(dt_bias.astype(F32))
    return w_main, w_ab, alog, dtb


def gdn_in_proj(x, mods, layer, gain, gw, tm):
    B, S, D = x.shape
    w_main, w_ab, alog, dtb = gw
    n_slabs = w_main.shape[1] // LANES
    out_specs, out_shape = _gdn_in_outputs(B, S, tm, n_slabs)
    vmem = (2 * tm * D * 4 + w_main.size * 2 + w_ab.size * 2 + 2 * tm * n_slabs * LANES * 2
            + 4 * tm * LANES * 4 + 3 * tm * 8 * LANES * 4 + (6 << 20))
    return pl.pallas_call(
        functools.partial(_gdn_in_kernel, chunk=GDN_CHUNK),
        grid=(B, S // tm),
        in_specs=[pl.BlockSpec((None, tm, D), lambda b, s: (b, s, 0)),
                  _resident((1, D)),
                  _mod_spec(D, layer, 0), _mod_spec(D, layer, 1),
                  _resident(w_main.shape), _resident(w_ab.shape),
                  _resident(alog.shape), _resident(dtb.shape)],
        out_specs=out_specs, out_shape=out_shape,
        compiler_params=_cparams(("arbitrary", "arbitrary"), vmem),
        name="gdn_in_proj",
    )(x, gain.reshape(1, D), mods, mods, w_main, w_ab, alog, dtb)


def _unit_lower_inverse(lmat, ri, ci):
    n = lmat.shape[0]
    diff = ri ^ ci
    p = jnp.where(ri == ci, 1.0, 0.0) - jnp.where(diff < 2, lmat, 0.0)
    s = 2
    while s < n:
        off = jnp.where(diff >= s, jnp.where(diff < 2 * s, lmat, 0.0), 0.0)
        p = p - _bdot(p, _bdot(off, p))
        s *= 2
    return p


def _gdn_core_kernel(q_ref, k_ref, v_ref, z_ref, gb_ref, gbt_ref, wq_ref, wk_ref, wv_ref, onorm_ref,
                     o_ref, state_ref, conv_ref, *, rows, chunk):
    head = pl.program_id(1)

    @pl.when(pl.program_id(2) == 0)
    def _():
        state_ref[...] = jnp.zeros_like(state_ref)
        conv_ref[:, 0:CONV_HALO, :] = jnp.zeros((3, CONV_HALO, LANES), F32)

    def conv_silu(idx, x_ref, w_ref):
        conv_ref[idx, CONV_HALO:CONV_HALO + rows, :] = x_ref[...].astype(F32)
        w = w_ref[...]
        acc = None
        for j in range(GDN_CONV):
            off = CONV_HALO - (GDN_CONV - 1) + j
            term = w[j:j + 1, :] * conv_ref[idx, off:off + rows, :]
            acc = term if acc is None else acc + term
        conv_ref[idx, 0:CONV_HALO, :] = conv_ref[idx, rows:rows + CONV_HALO, :]
        return _silu(acc)

    def l2n(t):
        return t * lax.rsqrt(jnp.sum(t * t, axis=-1, keepdims=True) + NORM_EPS)

    q = l2n(conv_silu(0, q_ref, wq_ref)) * (GDN_DK ** -0.5)
    k = l2n(conv_silu(1, k_ref, wk_ref))
    v = conv_silu(2, v_ref, wv_ref)

    gb = gb_ref[...]
    lane = lax.broadcasted_iota(jnp.int32, gb.shape, 1)
    gc_col = jnp.sum(jnp.where(lane == head, gb, 0.0), axis=1, keepdims=True)
    beta_col = jnp.sum(jnp.where(lane == head + GDN_HEADS, gb, 0.0), axis=1, keepdims=True)
    gc_row = gbt_ref[pl.ds(head, 1), :]

    ri = lax.broadcasted_iota(jnp.int32, (chunk, chunk), 0)
    ci = lax.broadcasted_iota(jnp.int32, (chunk, chunk), 1)
    onorm = onorm_ref[...]

    for c in range(rows // chunk):
        sl = slice(c * chunk, (c + 1) * chunk)
        qc, kc, vc = q[sl], k[sl], v[sl]
        gcc, bc = gc_col[sl], beta_col[sl]
        grc = gc_row[:, c * chunk:(c + 1) * chunk]
        g_last = gcc[chunk - 1:chunk, :]
        eg = jnp.exp(gcc)
        decay = jnp.exp(jnp.where(ri >= ci, gcc - grc, -jnp.inf))
        kb = kc * bc
        kk_qk = _bdot_nt(jnp.concatenate([kb, qc], axis=0), kc)
        lmat = jnp.where(ri > ci, kk_qk[:chunk] * decay, 0.0)
        attn = kk_qk[chunk:] * decay
        tinv = _unit_lower_inverse(lmat, ri, ci)
        uw = _bdot(tinv, jnp.concatenate([vc * bc, kb * eg], axis=1))
        u, w = uw[:, :LANES], uw[:, LANES:]
        state = state_ref[...]
        ws_qs = _bdot(jnp.concatenate([w, qc * eg], axis=0), state)
        v_new = u - ws_qs[:chunk]
        o = ws_qs[chunk:] + _bdot(attn, v_new)
        kg = kc * jnp.exp(g_last - gcc)
        state_ref[...] = state * jnp.exp(g_last) + _bdot(kg.T, v_new)
        zc = z_ref[sl, :].astype(F32)
        o_ref[sl, :] = (_rms(o) * onorm * _silu(zc)).astype(BF16)


def gdn_core(p, gb, gbt, conv_w, onorm, rows):
    B, _, S, _ = p.shape
    H = GDN_HEADS

    def slab(base):
        return pl.BlockSpec((None, None, rows, LANES), lambda b, h, s: (b, base + h, s, 0))

    def conv_spec(base):
        return pl.BlockSpec((GDN_CONV, LANES), lambda b, h, s: (0, base + h))

    return pl.pallas_call(
        functools.partial(_gdn_core_kernel, rows=rows, chunk=GDN_CHUNK),
        grid=(B, H, S // rows),
        in_specs=[slab(0), slab(H), slab(2 * H), slab(3 * H),
                  pl.BlockSpec((None, rows, LANES), lambda b, h, s: (b, s, 0)),
                  pl.BlockSpec((None, 2 * H, rows), lambda b, h, s: (b, 0, s)),
                  conv_spec(0), conv_spec(H), conv_spec(2 * H),
                  pl.BlockSpec((1, LANES), lambda b, h, s: (0, 0))],
        out_specs=pl.BlockSpec((None, None, rows, LANES), lambda b, h, s: (b, h, s, 0)),
        out_shape=jax.ShapeDtypeStruct((B, H, S, LANES), BF16),
        scratch_shapes=[pltpu.VMEM((GDN_DK, LANES), F32),
                        pltpu.VMEM((3, rows + CONV_HALO, LANES), F32)],
        compiler_params=_cparams(("arbitrary", "arbitrary", "arbitrary"), 24 << 20),
        name="gdn_core",
    )(p, p, p, p, gb, gbt, conv_w, conv_w, conv_w, onorm.reshape(1, LANES))


def _swa_kernel(sink_ref, q_ref, kp_ref, kc_ref, vp_ref, vc_ref, o_ref, *, layer_slot):
    blk = pl.program_id(1)
    W = SWA_WINDOW
    hd = SWA_HEAD_DIM
    group = SWA_Q_HEADS // SWA_KV_HEADS
    first_key = jnp.where(blk > 0, 0, W)
    kcat = jnp.concatenate([kp_ref[...], kc_ref[...]], axis=0)
    vcat = jnp.concatenate([vp_ref[...], vc_ref[...]], axis=0)
    lane = lax.broadcasted_iota(jnp.int32, (2 * W, LANES), 1)
    low = lane < hd
    ti = lax.broadcasted_iota(jnp.int32, (W, 2 * W), 0) + W
    si = lax.broadcasted_iota(jnp.int32, (W, 2 * W), 1)
    dist = ti - si
    valid = (dist >= 0) & (dist < SWA_WINDOW) & (si >= first_key)
    distf = dist.astype(F32)
    olane_low = lax.broadcasted_iota(jnp.int32, (W, LANES), 1) < hd
    zero = jnp.zeros((), BF16)

    for j in range(SWA_KV_HEADS):
        tile, half = divmod(j, LANES // hd)
        kt = kcat[:, tile * LANES:(tile + 1) * LANES]
        vt = vcat[:, tile * LANES:(tile + 1) * LANES]
        keep = low if half == 0 else ~low
        k_native = jnp.where(keep, kt, zero)
        v_native = jnp.where(keep, vt, zero)
        k_other = jnp.concatenate([k_native[:, hd:], k_native[:, :hd]], axis=1)
        v_other = jnp.concatenate([v_native[:, hd:], v_native[:, :hd]], axis=1)
        k_half = (k_native, k_other) if half == 0 else (k_other, k_native)
        v_half = (v_native, v_other) if half == 0 else (v_other, v_native)
        for pair in range(group // 2):
            qtile_idx = (j * group) // 2 + pair
            qt = q_ref[:, qtile_idx * LANES:(qtile_idx + 1) * LANES]
            acc = None
            for hh in range(2):
                head = j * group + pair * 2 + hh
                slope = 2.0 ** (-8.0 * (head + 1) / SWA_Q_HEADS)
                sink = sink_ref[layer_slot, head]
                s = lax.dot_general(qt, k_half[hh], (((1,), (1,)), ((), ())),
                                    preferred_element_type=F32) * (hd ** -0.5)
                s = jnp.where(valid, s - slope * distf, -jnp.inf)
                m = jnp.maximum(jnp.max(s, axis=-1, keepdims=True), sink)
                p = jnp.exp(s - m)
                denom = jnp.sum(p, axis=-1, keepdims=True) + jnp.exp(sink - m)
                p = p / denom
                part = jnp.dot(p.astype(BF16), v_half[hh], preferred_element_type=F32)
                acc = part if acc is None else acc + part
            o_ref[:, qtile_idx * LANES:(qtile_idx + 1) * LANES] = acc.astype(BF16)


def swa_attention(q, k_sh, v_sh, sinks, layer_slot):
    B, S, QW = q.shape
    KW = k_sh.shape[-1]
    W = SWA_WINDOW

    def cur(width):
        return pl.BlockSpec((None, W, width), lambda b, n: (b, n, 0))

    def prev(width):
        return pl.BlockSpec((None, W, width), lambda b, n: (b, jnp.maximum(n - 1, 0), 0))

    return pl.pallas_call(
        functools.partial(_swa_kernel, layer_slot=layer_slot),
        grid=(B, S // W),
        in_specs=[pl.BlockSpec(memory_space=pltpu.SMEM),
                  cur(QW), prev(KW), cur(KW), prev(KW), cur(KW)],
        out_specs=cur(QW),
        out_shape=jax.ShapeDtypeStruct((B, S, QW), BF16),
        compiler_params=_cparams(("arbitrary", "arbitrary"), 24 << 20),
        name="swa_attention",
    )(sinks, q, k_sh, k_sh, v_sh, v_sh)


def _post_mlp_kernel(*refs, head_major, tail, ff_chunk, gdn_chunk):
    (x_ref, a_ref, wo_ref, w1_ref, w2_ref, gains_ref,
     gate_mix_ref, shift_mlp_ref, scale_mlp_ref, gate_mlp_ref) = refs[:10]
    rest = refs[10:]
    x = x_ref[...]
    if head_major:
        act = jnp.concatenate([a_ref[h] for h in range(a_ref.shape[0])], axis=1)
    else:
        act = a_ref[...]
    y = jnp.dot(act, wo_ref[...], preferred_element_type=F32)
    x = x + (1.0 + gate_mix_ref[...]) * (_rms(y) * gains_ref[1:2, :])
    h = _adaln_pre(x, gains_ref[2:3, :], shift_mlp_ref[...], scale_mlp_ref[...]).astype(BF16)
    d_ff = w1_ref.shape[1]
    acc = None
    for f0 in range(0, d_ff, ff_chunk):
        a = jnp.dot(h, w1_ref[:, f0:f0 + ff_chunk], preferred_element_type=F32)
        a = jnp.square(jnp.maximum(a, 0.0)).astype(BF16)
        part = jnp.dot(a, w2_ref[f0:f0 + ff_chunk, :], preferred_element_type=F32)
        acc = part if acc is None else acc + part
    x = x + (1.0 + gate_mlp_ref[...]) * (_rms(acc) * gains_ref[3:4, :])

    if tail == "none":
        (xo_ref,) = rest
        xo_ref[...] = x
    elif tail == "gdn_in":
        (ngain_ref, nshift_ref, nscale_ref, w_ref, wab_ref, alog_ref, dtb_ref,
         xo_ref, p_ref, gb_ref, gbt_ref) = rest
        xo_ref[...] = x
        _gdn_in_body(x, ngain_ref, nshift_ref, nscale_ref, w_ref, wab_ref, alog_ref, dtb_ref,
                     p_ref, gb_ref, gbt_ref, gdn_chunk)
    elif tail == "kv_q":
        (kvgain_ref, kvshift_ref, kvscale_ref, wkv_ref, ngain_ref, nshift_ref, nscale_ref, wq_ref,
         xo_ref, k_ref, v_ref, q_ref) = rest
        xo_ref[...] = x
        hk = _adaln_pre(x, kvgain_ref[...], kvshift_ref[...], kvscale_ref[...]).astype(BF16)
        kv = jnp.dot(hk, wkv_ref[...], preferred_element_type=F32)
        kvw = kv.shape[1] // 2
        k_ref[...] = kv[:, :kvw].astype(BF16)
        v_ref[...] = kv[:, kvw:].astype(BF16)
        hq = _adaln_pre(x, ngain_ref[...], nshift_ref[...], nscale_ref[...]).astype(BF16)
        q_ref[...] = jnp.dot(hq, wq_ref[...], preferred_element_type=F32).astype(BF16)
    elif tail == "q":
        (ngain_ref, nshift_ref, nscale_ref, wq_ref, xo_ref, q_ref) = rest
        xo_ref[...] = x
        hq = _adaln_pre(x, ngain_ref[...], nshift_ref[...], nscale_ref[...]).astype(BF16)
        q_ref[...] = jnp.dot(hq, wq_ref[...], preferred_element_type=F32).astype(BF16)
    else:
        raise ValueError(tail)


def post_mlp(x, act, mods, layer, gains, w_o, w1, w2, tm, *, tail="none", tail_args=None):
    B, S, D = x.shape
    head_major = act.ndim == 4
    row = lambda width: pl.BlockSpec((None, tm, width), lambda b, s: (b, s, 0))
    if head_major:
        act_spec = pl.BlockSpec((None, act.shape[1], tm, LANES), lambda b, s: (b, 0, s, 0))
    else:
        act_spec = row(act.shape[-1])
    ins = [x, act, w_o, w1, w2, gains, mods, mods, mods, mods]
    in_specs = [row(D), act_spec, _resident(w_o.shape), _resident(w1.shape), _resident(w2.shape),
                _resident(gains.shape),
                _mod_spec(D, layer, 2), _mod_spec(D, layer, 3), _mod_spec(D, layer, 4), _mod_spec(D, layer, 5)]
    out_specs = [row(D)]
    out_shape = [jax.ShapeDtypeStruct((B, S, D), F32)]
    weight_bytes = (w_o.size + w1.size + w2.size) * 2
    if tail == "gdn_in":
        nlayer, ngain, (w_main, w_ab, alog, dtb) = tail_args
        ins += [ngain.reshape(1, D), mods, mods, w_main, w_ab, alog, dtb]
        in_specs += [_resident((1, D)), _mod_spec(D, nlayer, 0), _mod_spec(D, nlayer, 1),
                     _resident(w_main.shape), _resident(w_ab.shape), _resident(alog.shape),
                     _resident(dtb.shape)]
        n_slabs = w_main.shape[1] // LANES
        os_, osh = _gdn_in_outputs(B, S, tm, n_slabs)
        out_specs += os_
        out_shape += osh
        weight_bytes += (w_main.size + w_ab.size) * 2
    elif tail == "kv_q":
        kvmods, kvgain, w_kv, nlayer, ngain, w_q = tail_args
        kv_spec = lambda k: pl.BlockSpec((None, None, 1, D), lambda b, s: (0, b, 0, k))
        ins += [kvgain.reshape(1, D), kvmods, kvmods, w_kv, ngain.reshape(1, D), mods, mods, w_q]
        in_specs += [_resident((1, D)), kv_spec(0), kv_spec(1), _resident(w_kv.shape),
                     _resident((1, D)), _mod_spec(D, nlayer, 0), _mod_spec(D, nlayer, 1),
                     _resident(w_q.shape)]
        kvw = w_kv.shape[1] // 2
        out_specs += [row(kvw), row(kvw), row(w_q.shape[1])]
        out_shape += [jax.ShapeDtypeStruct((B, S, kvw), BF16), jax.ShapeDtypeStruct((B, S, kvw), BF16),
                      jax.ShapeDtypeStruct((B, S, w_q.shape[1]), BF16)]
        weight_bytes += (w_kv.size + w_q.size) * 2
    elif tail == "q":
        nlayer, ngain, w_q = tail_args
        ins += [ngain.reshape(1, D), mods, mods, w_q]
        in_specs += [_resident((1, D)), _mod_spec(D, nlayer, 0), _mod_spec(D, nlayer, 1),
                     _resident(w_q.shape)]
        out_specs += [row(w_q.shape[1])]
        out_shape += [jax.ShapeDtypeStruct((B, S, w_q.shape[1]), BF16)]
        weight_bytes += w_q.size * 2
    ff_chunk = 512
    vmem = weight_bytes + 8 * tm * D * 4 + 4 * tm * ff_chunk * 4 + 12 * tm * D * 2 + (8 << 20)
    return pl.pallas_call(
        functools.partial(_post_mlp_kernel, head_major=head_major, tail=tail, ff_chunk=ff_chunk,
                          gdn_chunk=GDN_CHUNK),
        grid=(B, S // tm),
        in_specs=in_specs, out_specs=out_specs, out_shape=out_shape,
        compiler_params=_cparams(("arbitrary", "arbitrary"), vmem),
        name="post_mlp_" + tail,
    )(*ins)


def kernel(x, c, mod_w, mod_b, norm_g, gdn_w_in, gdn_conv, gdn_a_log, gdn_dt_bias, gdn_onorm,
           gdn_w_out, kv_mod_w, kv_mod_b, kv_norm, w_kv, swa_w_q, swa_sinks, swa_w_o,
           mlp_w1, mlp_w2):
    B, S, D = x.shape
    depth = mod_w.shape[0]
    n_gdn = gdn_w_in.shape[0]
    tm = min(S, 512)
    rows = min(S, 256)
    assert S % tm == 0 and S % rows == 0 and rows % GDN_CHUNK == 0 and S % SWA_WINDOW == 0

    mods = modulation(c, mod_w, mod_b).reshape(depth, B, 1, mod_w.shape[-1])
    kvmods = modulation(c, kv_mod_w[None], kv_mod_b[None]).reshape(1, B, 1, kv_mod_w.shape[-1])

    w1 = mlp_w1.astype(BF16)
    w2 = mlp_w2.astype(BF16)
    gdn_w = [_gdn_weights(gdn_w_in[i], gdn_a_log[i], gdn_dt_bias[i]) for i in range(n_gdn)]
    gdn_wo = gdn_w_out.astype(BF16)
    swa_wq = swa_w_q.astype(BF16)
    swa_wo = swa_w_o.astype(BF16)
    wkv = w_kv.astype(BF16)
    sinks = swa_sinks.astype(F32)

    p, gb, gbt = gdn_in_proj(x, mods, 0, norm_g[0, 0], gdn_w[0], tm)
    k_sh = v_sh = q = None
    for layer in range(depth):
        if layer < n_gdn:
            act = gdn_core(p, gb, gbt, gdn_conv[layer], gdn_onorm[layer], rows)
            w_o = gdn_wo[layer]
        else:
            act = swa_attention(q, k_sh, v_sh, sinks, layer - n_gdn)
            w_o = swa_wo[layer - n_gdn]
        nxt = layer + 1
        if nxt == depth:
            tail, tail_args = "none", None
        elif nxt < n_gdn:
            tail, tail_args = "gdn_in", (nxt, norm_g[nxt, 0], gdn_w[nxt])
        elif nxt == n_gdn:
            tail, tail_args = "kv_q", (kvmods, kv_norm, wkv, nxt, norm_g[nxt, 0], swa_wq[0])
        else:
            tail, tail_args = "q", (nxt, norm_g[nxt, 0], swa_wq[nxt - n_gdn])
        outs = post_mlp(x, act, mods, layer, norm_g[layer], w_o, w1[layer], w2[layer], tm,
                        tail=tail, tail_args=tail_args)
        x = outs[0]
        if tail == "gdn_in":
            p, gb, gbt = outs[1:]
        elif tail == "kv_q":
            k_sh, v_sh, q = outs[1:]
        elif tail == "q":
            (q,) = outs[1:]
    return x
```

```python
import functools
import math

import jax
import jax.numpy as jnp
from jax import lax
from jax.experimental import pallas as pl
from jax.experimental.pallas import tpu as pltpu

F32 = jnp.float32
BF16 = jnp.bfloat16

NORM_EPS = 1e-6
LANES = 128
GDN_HEADS = 8
GDN_DK = 128
GDN_CONV = 4
SWA_HEAD_DIM = 64
SWA_Q_HEADS = 16
SWA_KV_HEADS = 4
SWA_WINDOW = 128
V7X_SCOPED_VMEM_BYTES = 60000 * 1024

GDN_CHUNK = 64
CONV_HALO = 8


def _cparams(semantics, vmem_bytes):
    return pltpu.CompilerParams(dimension_semantics=semantics,
                                vmem_limit_bytes=min(int(vmem_bytes), V7X_SCOPED_VMEM_BYTES))


def _sigmoid(x):
    return 1.0 / (1.0 + jnp.exp(-x))


def _silu(x):
    return x * _sigmoid(x)


def _rms(x):
    return x * lax.rsqrt(jnp.mean(x * x, axis=-1, keepdims=True) + NORM_EPS)


def _bdot(a, b):
    return jnp.dot(a.astype(BF16), b.astype(BF16), preferred_element_type=F32)


def _bmm(a, b):
    return jnp.einsum("bij,bjk->bik", a.astype(BF16), b.astype(BF16), preferred_element_type=F32)


def _bdot_nt(a, b):
    return lax.dot_general(a.astype(BF16), b.astype(BF16), (((1,), (1,)), ((), ())),
                           preferred_element_type=F32)


def _resident(shape):
    nd = len(shape)
    return pl.BlockSpec(shape, lambda *_: (0,) * nd, pipeline_mode=pl.Buffered(1))


def _mod_kernel(c_ref, w_ref, b_ref, o_ref):
    o_ref[0] = _bdot(_silu(c_ref[...]), w_ref[0]) + b_ref[0]


def modulation(c, w, b):
    L, D, N = w.shape
    B = c.shape[0]
    tn = min(N, 1024)
    assert N % tn == 0
    return pl.pallas_call(
        _mod_kernel,
        grid=(L, N // tn),
        in_specs=[pl.BlockSpec((B, D), lambda l, n: (0, 0)),
                  pl.BlockSpec((1, D, tn), lambda l, n: (l, 0, n)),
                  pl.BlockSpec((1, 1, tn), lambda l, n: (l, 0, n))],
        out_specs=pl.BlockSpec((1, B, tn), lambda l, n: (l, 0, n)),
        out_shape=jax.ShapeDtypeStruct((L, B, N), F32),
        compiler_params=_cparams(("arbitrary", "arbitrary"), 4 * D * tn * 4 + (4 << 20)),
        name="modulation",
    )(c, w, b.reshape(L, 1, N))


def _mod_spec(D, layer, k):
    return pl.BlockSpec((None, None, 1, D), lambda b, *_: (layer, b, 0, k))


def _adaln_pre(x, gain, shift, scale):
    return _rms(x) * gain * (1.0 + scale) + shift


def _gdn_gates(ab, alog_ref, dtb_ref, out_gb_ref, out_gbt_ref, chunk):
    tm = ab.shape[0]
    lane = lax.broadcasted_iota(jnp.int32, ab.shape, 1)
    pre = ab + dtb_ref[...]
    softplus = jnp.maximum(pre, 0.0) + jnp.log(1.0 + jnp.exp(-jnp.abs(pre)))
    g = -jnp.exp(alog_ref[...]) * softplus
    g = jnp.where(lane < GDN_HEADS, g, 0.0)
    beta = _sigmoid(ab)
    r = lax.broadcasted_iota(jnp.int32, (chunk, chunk), 0)
    c = lax.broadcasted_iota(jnp.int32, (chunk, chunk), 1)
    tri = (r >= c).astype(F32)
    parts = []
    for i in range(tm // chunk):
        parts.append(jnp.dot(tri, g[i * chunk:(i + 1) * chunk], preferred_element_type=F32,
                             precision=lax.Precision.HIGHEST))
    gcum = jnp.concatenate(parts, axis=0) if len(parts) > 1 else parts[0]
    gb = jnp.where(lane < GDN_HEADS, gcum, jnp.where(lane < 2 * GDN_HEADS, beta, 0.0))
    out_gb_ref[...] = gb
    out_gbt_ref[...] = gb.T[:2 * GDN_HEADS]


def _gdn_in_body(x, gain_ref, shift_ref, scale_ref, w_ref, wab_ref, alog_ref, dtb_ref,
                 p_ref, gb_ref, gbt_ref, chunk):
    h = _adaln_pre(x, gain_ref[...], shift_ref[...], scale_ref[...]).astype(BF16)
    n_slabs = w_ref.shape[1] // LANES
    group = 8
    for j0 in range(0, n_slabs, group):
        res = jnp.dot(h, w_ref[:, j0 * LANES:(j0 + group) * LANES], preferred_element_type=F32)
        for j in range(group):
            p_ref[j0 + j] = res[:, j * LANES:(j + 1) * LANES].astype(BF16)
    ab = jnp.dot(h, wab_ref[...], preferred_element_type=F32)
    _gdn_gates(ab, alog_ref, dtb_ref, gb_ref, gbt_ref, chunk)


def _gdn_in_kernel(x_ref, gain_ref, shift_ref, scale_ref, w_ref, wab_ref, alog_ref, dtb_ref,
                   p_ref, gb_ref, gbt_ref, *, chunk):
    _gdn_in_body(x_ref[...], gain_ref, shift_ref, scale_ref, w_ref, wab_ref, alog_ref, dtb_ref,
                 p_ref, gb_ref, gbt_ref, chunk)


def _gdn_in_outputs(B, S, tm, n_slabs):
    out_specs = [pl.BlockSpec((None, n_slabs, tm, LANES), lambda b, s: (b, 0, s, 0)),
                 pl.BlockSpec((None, tm, LANES), lambda b, s: (b, s, 0)),
                 pl.BlockSpec((None, 2 * GDN_HEADS, tm), lambda b, s: (b, 0, s))]
    out_shape = [jax.ShapeDtypeStruct((B, n_slabs, S, LANES), BF16),
                 jax.ShapeDtypeStruct((B, S, LANES), F32),
                 jax.ShapeDtypeStruct((B, 2 * GDN_HEADS, S), F32)]
    return out_specs, out_shape


def _gdn_weights(w_in, a_log, dt_bias):
    D, n_in = w_in.shape
    main = n_in - 2 * GDN_HEADS
    w_main = w_in[:, :main].astype(BF16)
    w_ab = jnp.zeros((D, LANES), F32).at[:, :2 * GDN_HEADS].set(w_in[:, main:]).astype(BF16)
    alog = jnp.zeros((1, LANES), F32).at[0, :GDN_HEADS].set(a_log.astype(F32))
    dtb = jnp.zeros((1, LANES), F32).at[0, :GDN_HEADS].set(dt_bias.astype(F32))
    return w_main, w_ab, alog, dtb


def gdn_in_proj(x, mods, layer, gain, gw, tm):
    B, S, D = x.shape
    w_main, w_ab, alog, dtb = gw
    n_slabs = w_main.shape[1] // LANES
    out_specs, out_shape = _gdn_in_outputs(B, S, tm, n_slabs)
    vmem = (2 * tm * D * 4 + w_main.size * 2 + w_ab.size * 2 + 2 * tm * n_slabs * LANES * 2
            + 4 * tm * LANES * 4 + 3 * tm * 8 * LANES * 4 + (6 << 20))
    return pl.pallas_call(
        functools.partial(_gdn_in_kernel, chunk=GDN_CHUNK),
        grid=(B, S // tm),
        in_specs=[pl.BlockSpec((None, tm, D), lambda b, s: (b, s, 0)),
                  _resident((1, D)),
                  _mod_spec(D, layer, 0), _mod_spec(D, layer, 1),
                  _resident(w_main.shape), _resident(w_ab.shape),
                  _resident(alog.shape), _resident(dtb.shape)],
        out_specs=out_specs, out_shape=out_shape,
        compiler_params=_cparams(("arbitrary", "arbitrary"), vmem),
        name="gdn_in_proj",
    )(x, gain.reshape(1, D), mods, mods, w_main, w_ab, alog, dtb)


def _unit_lower_inverse(lmat, ri, ci):
    n = lmat.shape[-1]
    diff = ri ^ ci
    p = jnp.where(ri == ci, 1.0, 0.0) - jnp.where(diff < 2, lmat, 0.0)
    s = 2
    while s < n:
        off = jnp.where(diff >= s, jnp.where(diff < 2 * s, lmat, 0.0), 0.0)
        p = p - _bmm(p, _bmm(off, p))
        s *= 2
    return p


def _gdn_core_kernel(p_ref, gb_ref, gbt_ref, convw_ref, onorm_ref, o_ref, state_ref, conv_ref,
                     *, rows, chunk):
    H = GDN_HEADS
    nc = rows // chunk

    @pl.when(pl.program_id(1) == 0)
    def _():
        state_ref[...] = jnp.zeros_like(state_ref)
        conv_ref[:, 0:CONV_HALO, :] = jnp.zeros((3 * H, CONV_HALO, LANES), F32)

    conv_ref[:, CONV_HALO:CONV_HALO + rows, :] = p_ref[0:3 * H].astype(F32)
    acc = None
    for j in range(GDN_CONV):
        off = CONV_HALO - (GDN_CONV - 1) + j
        term = convw_ref[j] * conv_ref[:, off:off + rows, :]
        acc = term if acc is None else acc + term
    conv_ref[:, 0:CONV_HALO, :] = conv_ref[:, rows:rows + CONV_HALO, :]
    qkv = _silu(acc)
    qk = qkv[0:2 * H]
    qk = qk * lax.rsqrt(jnp.sum(qk * qk, axis=-1, keepdims=True) + NORM_EPS)

    def chunked(t):
        if nc == 1:
            return t
        return jnp.concatenate([t[:, c * chunk:(c + 1) * chunk] for c in range(nc)], axis=0)

    q = chunked(qk[0:H]) * (GDN_DK ** -0.5)
    k = chunked(qk[H:2 * H])
    v = chunked(qkv[2 * H:3 * H])

    gb = gb_ref[...]
    gbt = gbt_ref[...]
    gcc = chunked(jnp.stack([gb[:, h:h + 1] for h in range(H)]))
    bc = chunked(jnp.stack([gb[:, H + h:H + h + 1] for h in range(H)]))
    grc = jnp.stack([gbt[h:h + 1, c * chunk:(c + 1) * chunk]
                     for c in range(nc) for h in range(H)])
    g_last = gcc[:, chunk - 1:chunk, :]

    ri = lax.broadcasted_iota(jnp.int32, (1, chunk, chunk), 1)
    ci = lax.broadcasted_iota(jnp.int32, (1, chunk, chunk), 2)
    eg = jnp.exp(gcc)
    decay = jnp.exp(jnp.where(ri >= ci, gcc - grc, -jnp.inf))
    kb = k * bc
    kk_qk = jnp.einsum("bcd,bsd->bcs", jnp.concatenate([kb, q], axis=1).astype(BF16), k.astype(BF16),
                       preferred_element_type=F32)
    lmat = jnp.where(ri > ci, kk_qk[:, :chunk] * decay, 0.0)
    attn = (kk_qk[:, chunk:] * decay).astype(BF16)
    tinv = _unit_lower_inverse(lmat, ri, ci)
    uw = _bmm(tinv, jnp.concatenate([v * bc, kb * eg], axis=2))
    u, w = uw[:, :, :LANES], uw[:, :, LANES:]
    wq = jnp.concatenate([w, q * eg], axis=1).astype(BF16)
    kg = k * jnp.exp(g_last - gcc)
    kgt = jnp.stack([kg[b].T for b in range(nc * H)]).astype(BF16)
    egl = jnp.exp(g_last)
    onorm = onorm_ref[...]

    state = state_ref[...]
    for c in range(nc):
        sel = slice(c * H, (c + 1) * H)
        ws_qs = _bmm(wq[sel], state)
        v_new = (u[sel] - ws_qs[:, :chunk]).astype(BF16)
        o = ws_qs[:, chunk:] + _bmm(attn[sel], v_new)
        state = state * egl[sel] + _bmm(kgt[sel], v_new)
        zc = p_ref[3 * H:4 * H, c * chunk:(c + 1) * chunk, :].astype(F32)
        o_ref[:, c * chunk:(c + 1) * chunk, :] = (_rms(o) * onorm * _silu(zc)).astype(BF16)
    state_ref[...] = state


def gdn_core(p, gb, gbt, conv_w, onorm, rows):
    B, n_slabs, S, _ = p.shape
    H = GDN_HEADS
    return pl.pallas_call(
        functools.partial(_gdn_core_kernel, rows=rows, chunk=GDN_CHUNK),
        grid=(B, S // rows),
        in_specs=[pl.BlockSpec((None, n_slabs, rows, LANES), lambda b, s: (b, 0, s, 0)),
                  pl.BlockSpec((None, rows, LANES), lambda b, s: (b, s, 0)),
                  pl.BlockSpec((None, 2 * H, rows), lambda b, s: (b, 0, s)),
                  _resident((GDN_CONV, 3 * H, 1, LANES)),
                  _resident((1, LANES))],
        out_specs=pl.BlockSpec((None, H, rows, LANES), lambda b, s: (b, 0, s, 0)),
        out_shape=jax.ShapeDtypeStruct((B, H, S, LANES), BF16),
        scratch_shapes=[pltpu.VMEM((H, GDN_DK, LANES), F32),
                        pltpu.VMEM((3 * H, rows + CONV_HALO, LANES), F32)],
        compiler_params=_cparams(("arbitrary", "arbitrary"), 32 << 20),
        name="gdn_core",
    )(p, gb, gbt, conv_w.reshape(GDN_CONV, 3 * H, 1, LANES), onorm.reshape(1, LANES))


def _swa_kernel(sink_ref, q_ref, kp_ref, kc_ref, vp_ref, vc_ref, o_ref, *, layer_slot):
    blk = pl.program_id(1)
    W = SWA_WINDOW
    hd = SWA_HEAD_DIM
    group = SWA_Q_HEADS // SWA_KV_HEADS
    first_key = jnp.where(blk > 0, 0, W)
    kcat = jnp.concatenate([kp_ref[...], kc_ref[...]], axis=0)
    vcat = jnp.concatenate([vp_ref[...], vc_ref[...]], axis=0)
    lane = lax.broadcasted_iota(jnp.int32, (2 * W, LANES), 1)
    low = lane < hd
    ti = lax.broadcasted_iota(jnp.int32, (W, 2 * W), 0) + W
    si = lax.broadcasted_iota(jnp.int32, (W, 2 * W), 1)
    dist = ti - si
    valid = (dist >= 0) & (dist < SWA_WINDOW) & (si >= first_key)
    distf = dist.astype(F32)
    olane_low = lax.broadcasted_iota(jnp.int32, (W, LANES), 1) < hd
    zero = jnp.zeros((), BF16)

    for j in range(SWA_KV_HEADS):
        tile, half = divmod(j, LANES // hd)
        kt = kcat[:, tile * LANES:(tile + 1) * LANES]
        vt = vcat[:, tile * LANES:(tile + 1) * LANES]
        keep = low if half == 0 else ~low
        k_native = jnp.where(keep, kt, zero)
        v_native = jnp.where(keep, vt, zero)
        k_other = jnp.concatenate([k_native[:, hd:], k_native[:, :hd]], axis=1)
        v_other = jnp.concatenate([v_native[:, hd:], v_native[:, :hd]], axis=1)
        k_half = (k_native, k_other) if half == 0 else (k_other, k_native)
        v_half = (v_native, v_other) if half == 0 else (v_other, v_native)
        for pair in range(group // 2):
            qtile_idx = (j * group) // 2 + pair
            qt = q_ref[:, qtile_idx * LANES:(qtile_idx + 1) * LANES]
            acc = None
            for hh in range(2):
                head = j * group + pair * 2 + hh
                slope = 2.0 ** (-8.0 * (head + 1) / SWA_Q_HEADS)
                sink = sink_ref[layer_slot, head]
                s = lax.dot_general(qt, k_half[hh], (((1,), (1,)), ((), ())),
                                    preferred_element_type=F32) * (hd ** -0.5)
                s = jnp.where(valid, s - slope * distf, -jnp.inf)
                m = jnp.maximum(jnp.max(s, axis=-1, keepdims=True), sink)
                p = jnp.exp(s - m)
                denom = jnp.sum(p, axis=-1, keepdims=True) + jnp.exp(sink - m)
                p = p / denom
                part = jnp.dot(p.astype(BF16), v_half[hh], preferred_element_type=F32)
                acc = part if acc is None else acc + part
            o_ref[:, qtile_idx * LANES:(qtile_idx + 1) * LANES] = acc.astype(BF16)


def swa_attention(q, k_sh, v_sh, sinks, layer_slot):
    B, S, QW = q.shape
    KW = k_sh.shape[-1]
    W = SWA_WINDOW

    def cur(width):
        return pl.BlockSpec((None, W, width), lambda b, n: (b, n, 0))

    def prev(width):
        return pl.BlockSpec((None, W, width), lambda b, n: (b, jnp.maximum(n - 1, 0), 0))

    return pl.pallas_call(
        functools.partial(_swa_kernel, layer_slot=layer_slot),
        grid=(B, S // W),
        in_specs=[pl.BlockSpec(memory_space=pltpu.SMEM),
                  cur(QW), prev(KW), cur(KW), prev(KW), cur(KW)],
        out_specs=cur(QW),
        out_shape=jax.ShapeDtypeStruct((B, S, QW), BF16),
        compiler_params=_cparams(("arbitrary", "arbitrary"), 24 << 20),
        name="swa_attention",
    )(sinks, q, k_sh, k_sh, v_sh, v_sh)


def _post_mlp_kernel(*refs, head_major, tail, ff_chunk, gdn_chunk):
    (x_ref, a_ref, wo_ref, w1_ref, w2_ref, gains_ref,
     gate_mix_ref, shift_mlp_ref, scale_mlp_ref, gate_mlp_ref) = refs[:10]
    rest = refs[10:]
    x = x_ref[...]
    if head_major:
        act = jnp.concatenate([a_ref[h] for h in range(a_ref.shape[0])], axis=1)
    else:
        act = a_ref[...]
    y = jnp.dot(act, wo_ref[...], preferred_element_type=F32)
    x = x + (1.0 + gate_mix_ref[...]) * (_rms(y) * gains_ref[1:2, :])
    h = _adaln_pre(x, gains_ref[2:3, :], shift_mlp_ref[...], scale_mlp_ref[...]).astype(BF16)
    d_ff = w1_ref.shape[1]
    acc = None
    for f0 in range(0, d_ff, ff_chunk):
        a = jnp.dot(h, w1_ref[:, f0:f0 + ff_chunk], preferred_element_type=F32)
        a = jnp.square(jnp.maximum(a, 0.0)).astype(BF16)
        part = jnp.dot(a, w2_ref[f0:f0 + ff_chunk, :], preferred_element_type=F32)
        acc = part if acc is None else acc + part
    x = x + (1.0 + gate_mlp_ref[...]) * (_rms(acc) * gains_ref[3:4, :])

    if tail == "none":
        (xo_ref,) = rest
        xo_ref[...] = x
    elif tail == "gdn_in":
        (ngain_ref, nshift_ref, nscale_ref, w_ref, wab_ref, alog_ref, dtb_ref,
         xo_ref, p_ref, gb_ref, gbt_ref) = rest
        xo_ref[...] = x
        _gdn_in_body(x, ngain_ref, nshift_ref, nscale_ref, w_ref, wab_ref, alog_ref, dtb_ref,
                     p_ref, gb_ref, gbt_ref, gdn_chunk)
    elif tail == "kv_q":
        (kvgain_ref, kvshift_ref, kvscale_ref, wkv_ref, ngain_ref, nshift_ref, nscale_ref, wq_ref,
         xo_ref, k_ref, v_ref, q_ref) = rest
        xo_ref[...] = x
        hk = _adaln_pre(x, kvgain_ref[...], kvshift_ref[...], kvscale_ref[...]).astype(BF16)
        kv = jnp.dot(hk, wkv_ref[...], preferred_element_type=F32)
        kvw = kv.shape[1] // 2
        k_ref[...] = kv[:, :kvw].astype(BF16)
        v_ref[...] = kv[:, kvw:].astype(BF16)
        hq = _adaln_pre(x, ngain_ref[...], nshift_ref[...], nscale_ref[...]).astype(BF16)
        q_ref[...] = jnp.dot(hq, wq_ref[...], preferred_element_type=F32).astype(BF16)
    elif tail == "q":
        (ngain_ref, nshift_ref, nscale_ref, wq_ref, xo_ref, q_ref) = rest
        xo_ref[...] = x
        hq = _adaln_pre(x, ngain_ref[...], nshift_ref[...], nscale_ref[...]).astype(BF16)
        q_ref[...] = jnp.dot(hq, wq_ref[...], preferred_element_type=F32).astype(BF16)
    else:
        raise ValueError(tail)


def post_mlp(x, act, mods, layer, gains, w_o, w1, w2, tm, *, tail="none", tail_args=None):
    B, S, D = x.shape
    head_major = act.ndim == 4
    row = lambda width: pl.BlockSpec((None, tm, width), lambda b, s: (b, s, 0))
    if head_major:
        act_spec = pl.BlockSpec((None, act.shape[1], tm, LANES), lambda b, s: (b, 0, s, 0))
    else:
        act_spec = row(act.shape[-1])
    ins = [x, act, w_o, w1, w2, gains, mods, mods, mods, mods]
    in_specs = [row(D), act_spec, _resident(w_o.shape), _resident(w1.shape), _resident(w2.shape),
                _resident(gains.shape),
                _mod_spec(D, layer, 2), _mod_spec(D, layer, 3), _mod_spec(D, layer, 4), _mod_spec(D, layer, 5)]
    out_specs = [row(D)]
    out_shape = [jax.ShapeDtypeStruct((B, S, D), F32)]
    weight_bytes = (w_o.size + w1.size + w2.size) * 2
    if tail == "gdn_in":
        nlayer, ngain, (w_main, w_ab, alog, dtb) = tail_args
        ins += [ngain.reshape(1, D), mods, mods, w_main, w_ab, alog, dtb]
        in_specs += [_resident((1, D)), _mod_spec(D, nlayer, 0), _mod_spec(D, nlayer, 1),
                     _resident(w_main.shape), _resident(w_ab.shape), _resident(alog.shape),
                     _resident(dtb.shape)]
        n_slabs = w_main.shape[1] // LANES
        os_, osh = _gdn_in_outputs(B, S, tm, n_slabs)
        out_specs += os_
        out_shape += osh
        weight_bytes += (w_main.size + w_ab.size) * 2
    elif tail == "kv_q":
        kvmods, kvgain, w_kv, nlayer, ngain, w_q = tail_args
        kv_spec = lambda k: pl.BlockSpec((None, None, 1, D), lambda b, s: (0, b, 0, k))
        ins += [kvgain.reshape(1, D), kvmods, kvmods, w_kv, ngain.reshape(1, D), mods, mods, w_q]
        in_specs += [_resident((1, D)), kv_spec(0), kv_spec(1), _resident(w_kv.shape),
                     _resident((1, D)), _mod_spec(D, nlayer, 0), _mod_spec(D, nlayer, 1),
                     _resident(w_q.shape)]
        kvw = w_kv.shape[1] // 2
        out_specs += [row(kvw), row(kvw), row(w_q.shape[1])]
        out_shape += [jax.ShapeDtypeStruct((B, S, kvw), BF16), jax.ShapeDtypeStruct((B, S, kvw), BF16),
                      jax.ShapeDtypeStruct((B, S, w_q.shape[1]), BF16)]
        weight_bytes += (w_kv.size + w_q.size) * 2
    elif tail == "q":
        nlayer, ngain, w_q = tail_args
        ins += [ngain.reshape(1, D), mods, mods, w_q]
        in_specs += [_resident((1, D)), _mod_spec(D, nlayer, 0), _mod_spec(D, nlayer, 1),
                     _resident(w_q.shape)]
        out_specs += [row(w_q.shape[1])]
        out_shape += [jax.ShapeDtypeStruct((B, S, w_q.shape[1]), BF16)]
        weight_bytes += w_q.size * 2
    ff_chunk = 512
    vmem = weight_bytes + 8 * tm * D * 4 + 4 * tm * ff_chunk * 4 + 12 * tm * D * 2 + (8 << 20)
    return pl.pallas_call(
        functools.partial(_post_mlp_kernel, head_major=head_major, tail=tail, ff_chunk=ff_chunk,
                          gdn_chunk=GDN_CHUNK),
        grid=(B, S // tm),
        in_specs=in_specs, out_specs=out_specs, out_shape=out_shape,
        compiler_params=_cparams(("arbitrary", "arbitrary"), vmem),
        name="post_mlp_" + tail,
    )(*ins)


def kernel(x, c, mod_w, mod_b, norm_g, gdn_w_in, gdn_conv, gdn_a_log, gdn_dt_bias, gdn_onorm,
           gdn_w_out, kv_mod_w, kv_mod_b, kv_norm, w_kv, swa_w_q, swa_sinks, swa_w_o,
           mlp_w1, mlp_w2):
    B, S, D = x.shape
    depth = mod_w.shape[0]
    n_gdn = gdn_w_in.shape[0]
    tm = min(S, 512)
    rows = min(S, 128)
    assert S % tm == 0 and S % rows == 0 and rows % GDN_CHUNK == 0 and S % SWA_WINDOW == 0

    mods = modulation(c, mod_w, mod_b).reshape(depth, B, 1, mod_w.shape[-1])
    kvmods = modulation(c, kv_mod_w[None], kv_mod_b[None]).reshape(1, B, 1, kv_mod_w.shape[-1])

    w1 = mlp_w1.astype(BF16)
    w2 = mlp_w2.astype(BF16)
    gdn_w = [_gdn_weights(gdn_w_in[i], gdn_a_log[i], gdn_dt_bias[i]) for i in range(n_gdn)]
    gdn_wo = gdn_w_out.astype(BF16)
    swa_wq = swa_w_q.astype(BF16)
    swa_wo = swa_w_o.astype(BF16)
    wkv = w_kv.astype(BF16)
    sinks = swa_sinks.astype(F32)

    p, gb, gbt = gdn_in_proj(x, mods, 0, norm_g[0, 0], gdn_w[0], tm)
    k_sh = v_sh = q = None
    for layer in range(depth):
        if layer < n_gdn:
            act = gdn_core(p, gb, gbt, gdn_conv[layer], gdn_onorm[layer], rows)
            w_o = gdn_wo[layer]
        else:
            act = swa_attention(q, k_sh, v_sh, sinks, layer - n_gdn)
            w_o = swa_wo[layer - n_gdn]
        nxt = layer + 1
        if nxt == depth:
            tail, tail_args = "none", None
        elif nxt < n_gdn:
            tail, tail_args = "gdn_in", (nxt, norm_g[nxt, 0], gdn_w[nxt])
        elif nxt == n_gdn:
            tail, tail_args = "kv_q", (kvmods, kv_norm, wkv, nxt, norm_g[nxt, 0], swa_wq[0])
        else:
            tail, tail_args = "q", (nxt, norm_g[nxt, 0], swa_wq[nxt - n_gdn])
        outs = post_mlp(x, act, mods, layer, norm_g[layer], w_o, w1[layer], w2[layer], tm,
                        tail=tail, tail_args=tail_args)
        x = outs[0]
        if tail == "gdn_in":
            p, gb, gbt = outs[1:]
        elif tail == "kv_q":
            k_sh, v_sh, q = outs[1:]
        elif tail == "q":
            (q,) = outs[1:]
    return x
```

```python
import functools

import jax
import jax.numpy as jnp
from jax import lax
from jax.experimental import pallas as pl
from jax.experimental.pallas import tpu as pltpu

F32 = jnp.float32
BF16 = jnp.bfloat16

NORM_EPS = 1e-6
LANES = 128
SUBLANES = 8
GDN_HEADS = 8
GDN_DK = 128
GDN_CONV = 4
SWA_HEAD_DIM = 64
SWA_Q_HEADS = 16
SWA_KV_HEADS = 4
SWA_WINDOW = 128
V7X_SCOPED_VMEM_BYTES = 60000 * 1024

GDN_CHUNK = 64
GDN_SLAB_GROUP = GDN_HEADS
CONV_HALO = SUBLANES


def _cparams(semantics, vmem_bytes):
    return pltpu.CompilerParams(dimension_semantics=semantics,
                                vmem_limit_bytes=min(int(vmem_bytes), V7X_SCOPED_VMEM_BYTES))


def _sigmoid(x):
    return 1.0 / (1.0 + jnp.exp(-x))


def _silu(x):
    return x * _sigmoid(x)


def _rms(x):
    return x * lax.rsqrt(jnp.mean(x * x, axis=-1, keepdims=True) + NORM_EPS)


def _bdot(a, b):
    return jnp.dot(a.astype(BF16), b.astype(BF16), preferred_element_type=F32)


def _bmm(a, b):
    return jnp.einsum("bij,bjk->bik", a.astype(BF16), b.astype(BF16), preferred_element_type=F32)


def _resident(shape):
    nd = len(shape)
    return pl.BlockSpec(shape, lambda *_: (0,) * nd, pipeline_mode=pl.Buffered(1))


def _layer_resident(arr, layer, block=None):
    block = tuple(arr.shape[1:]) if block is None else tuple(block)
    nd = len(block)
    return pl.BlockSpec((None,) + block, lambda *_: (layer,) + (0,) * nd, pipeline_mode=pl.Buffered(1))


def _mod_kernel(c_ref, w_ref, b_ref, o_ref):
    o_ref[0] = _bdot(_silu(c_ref[...]), w_ref[0]) + b_ref[0]


def modulation(c, w, b):
    L, D, N = w.shape
    B = c.shape[0]
    tn = min(N, 1024)
    assert N % tn == 0
    return pl.pallas_call(
        _mod_kernel,
        grid=(L, N // tn),
        in_specs=[pl.BlockSpec((B, D), lambda l, n: (0, 0)),
                  pl.BlockSpec((1, D, tn), lambda l, n: (l, 0, n)),
                  pl.BlockSpec((1, 1, tn), lambda l, n: (l, 0, n))],
        out_specs=pl.BlockSpec((1, B, tn), lambda l, n: (l, 0, n)),
        out_shape=jax.ShapeDtypeStruct((L, B, N), F32),
        compiler_params=_cparams(("arbitrary", "arbitrary"), 4 * D * tn * 4 + (4 << 20)),
        name="modulation",
    )(c, w, b.reshape(L, 1, N))


def _mod_spec(D, layer, k):
    return pl.BlockSpec((None, None, 1, D), lambda b, *_: (layer, b, 0, k))


def _adaln_pre(x, gain, shift, scale):
    return _rms(x) * gain * (1.0 + scale) + shift


def _gdn_gates(ab, alog_ref, dtb_ref, out_gb_ref, out_gbt_ref, chunk):
    tm = ab.shape[0]
    lane = lax.broadcasted_iota(jnp.int32, ab.shape, 1)
    pre = ab + dtb_ref[...]
    softplus = jnp.maximum(pre, 0.0) + jnp.log(1.0 + jnp.exp(-jnp.abs(pre)))
    g = -jnp.exp(alog_ref[...]) * softplus
    g = jnp.where(lane < GDN_HEADS, g, 0.0)
    beta = _sigmoid(ab)
    r = lax.broadcasted_iota(jnp.int32, (chunk, chunk), 0)
    c = lax.broadcasted_iota(jnp.int32, (chunk, chunk), 1)
    tri = (r >= c).astype(F32)
    parts = []
    for i in range(tm // chunk):
        parts.append(jnp.dot(tri, g[i * chunk:(i + 1) * chunk], preferred_element_type=F32,
                             precision=lax.Precision.HIGHEST))
    gcum = jnp.concatenate(parts, axis=0) if len(parts) > 1 else parts[0]
    gb = jnp.where(lane < GDN_HEADS, gcum, jnp.where(lane < 2 * GDN_HEADS, beta, 0.0))
    out_gb_ref[...] = gb
    out_gbt_ref[...] = gb.T[:2 * GDN_HEADS]


def _gdn_in_body(x, first_row_block, gain, shift_ref, scale_ref, w_ref, wab_ref, alog_ref, dtb_ref,
                 convw_ref, p_ref, gb_ref, gbt_ref, stage_ref, carry_ref, chunk):
    tm = x.shape[0]
    group = GDN_SLAB_GROUP

    @pl.when(first_row_block)
    def _():
        carry_ref[...] = jnp.zeros_like(carry_ref)

    h = _adaln_pre(x, gain, shift_ref[...], scale_ref[...]).astype(BF16)
    n_slabs = w_ref.shape[1] // LANES
    for gi in range(n_slabs // group):
        g0 = gi * group
        res = jnp.dot(h, w_ref[:, g0 * LANES:(g0 + group) * LANES], preferred_element_type=F32)
        if gi == 3:
            for j in range(group):
                p_ref[g0 + j] = res[:, j * LANES:(j + 1) * LANES].astype(BF16)
            continue
        stage_ref[:, 0:CONV_HALO, :] = carry_ref[gi]
        for j in range(group):
            stage_ref[j, CONV_HALO:CONV_HALO + tm, :] = res[:, j * LANES:(j + 1) * LANES]
        acc = None
        for t in range(GDN_CONV):
            off = CONV_HALO - (GDN_CONV - 1) + t
            term = convw_ref[t, g0:g0 + group] * stage_ref[:, off:off + tm, :]
            acc = term if acc is None else acc + term
        carry_ref[gi] = stage_ref[:, tm:tm + CONV_HALO, :]
        y = _silu(acc)
        if gi < 2:
            y = y * lax.rsqrt(jnp.sum(y * y, axis=-1, keepdims=True) + NORM_EPS)
        if gi == 0:
            y = y * (GDN_DK ** -0.5)
        p_ref[g0:g0 + group] = y.astype(BF16)
    ab = jnp.dot(h, wab_ref[...], preferred_element_type=F32)
    _gdn_gates(ab, alog_ref, dtb_ref, gb_ref, gbt_ref, chunk)


def _gdn_in_kernel(x_ref, gains_ref, shift_ref, scale_ref, w_ref, wab_ref, alog_ref, dtb_ref, convw_ref,
                   p_ref, gb_ref, gbt_ref, stage_ref, carry_ref, *, chunk):
    _gdn_in_body(x_ref[...], pl.program_id(1) == 0, gains_ref[0:1, :], shift_ref, scale_ref, w_ref, wab_ref,
                 alog_ref, dtb_ref, convw_ref, p_ref, gb_ref, gbt_ref, stage_ref, carry_ref, chunk)


def _gdn_in_outputs(B, S, tm, n_slabs):
    out_specs = [pl.BlockSpec((None, n_slabs, tm, LANES), lambda b, s: (b, 0, s, 0)),
                 pl.BlockSpec((None, tm, LANES), lambda b, s: (b, s, 0)),
                 pl.BlockSpec((None, 2 * GDN_HEADS, tm), lambda b, s: (b, 0, s))]
    out_shape = [jax.ShapeDtypeStruct((B, n_slabs, S, LANES), BF16),
                 jax.ShapeDtypeStruct((B, S, LANES), F32),
                 jax.ShapeDtypeStruct((B, 2 * GDN_HEADS, S), F32)]
    return out_specs, out_shape


def _gdn_in_scratch(tm):
    return [pltpu.VMEM((GDN_SLAB_GROUP, tm + CONV_HALO, LANES), F32),
            pltpu.VMEM((3, GDN_SLAB_GROUP, CONV_HALO, LANES), F32)]


def _gdn_in_scratch_bytes(tm):
    return GDN_SLAB_GROUP * (tm + CONV_HALO) * LANES * 4 + 3 * GDN_SLAB_GROUP * CONV_HALO * LANES * 4


def _gdn_params(w_in, conv_w, a_log, dt_bias):
    n, D, n_in = w_in.shape
    main = n_in - 2 * GDN_HEADS
    w_all = w_in.astype(BF16)
    w_ab = jnp.zeros((n, D, LANES), F32).at[:, :, :2 * GDN_HEADS].set(w_in[:, :, main:]).astype(BF16)
    alog = jnp.zeros((n, 1, LANES), F32).at[:, 0, :GDN_HEADS].set(a_log.astype(F32))
    dtb = jnp.zeros((n, 1, LANES), F32).at[:, 0, :GDN_HEADS].set(dt_bias.astype(F32))
    convw = conv_w.astype(F32).reshape(n, GDN_CONV, main // LANES - GDN_HEADS, 1, LANES)
    return dict(w_all=w_all, main=main, w_ab=w_ab, alog=alog, dtb=dtb, convw=convw)


def _gdn_in_inputs(gp, i):
    D = gp["w_all"].shape[1]
    ins = [gp["w_all"], gp["w_ab"], gp["alog"], gp["dtb"], gp["convw"]]
    specs = [_layer_resident(gp["w_all"], i, (D, gp["main"])), _layer_resident(gp["w_ab"], i),
             _layer_resident(gp["alog"], i), _layer_resident(gp["dtb"], i), _layer_resident(gp["convw"], i)]
    weight_bytes = (D * gp["main"] + D * LANES) * 2 + gp["convw"][0].size * 4 * SUBLANES
    return ins, specs, weight_bytes


def gdn_in_proj(x, mods, layer, gains, gp, i, tm):
    B, S, D = x.shape
    n_slabs = gp["main"] // LANES
    out_specs, out_shape = _gdn_in_outputs(B, S, tm, n_slabs)
    w_ins, w_specs, weight_bytes = _gdn_in_inputs(gp, i)
    vmem = (2 * tm * D * 4 + weight_bytes + 2 * tm * n_slabs * LANES * 2 + _gdn_in_scratch_bytes(tm)
            + 6 * tm * GDN_SLAB_GROUP * LANES * 4 + (6 << 20))
    return pl.pallas_call(
        functools.partial(_gdn_in_kernel, chunk=GDN_CHUNK),
        grid=(B, S // tm),
        in_specs=[pl.BlockSpec((None, tm, D), lambda b, s: (b, s, 0)),
                  _layer_resident(gains, layer),
                  _mod_spec(D, layer, 0), _mod_spec(D, layer, 1)] + w_specs,
        out_specs=out_specs, out_shape=out_shape,
        scratch_shapes=_gdn_in_scratch(tm),
        compiler_params=_cparams(("arbitrary", "arbitrary"), vmem),
        name="gdn_in_proj",
    )(x, gains, mods, mods, *w_ins)


def _unit_lower_inverse(lmat, ri, ci):
    n = lmat.shape[-1]
    diff = ri ^ ci
    p = jnp.where(ri == ci, 1.0, 0.0) - jnp.where(diff < 2, lmat, 0.0)
    s = 2
    while s < n:
        off = jnp.where(diff >= s, jnp.where(diff < 2 * s, lmat, 0.0), 0.0)
        p = p - _bmm(p, _bmm(off, p))
        s *= 2
    return p


def _gdn_core_kernel(q_ref, k_ref, v_ref, gb_ref, gbt_ref, o_ref, state_ref, *, rows, chunk):
    H = GDN_HEADS
    nc = rows // chunk

    @pl.when(pl.program_id(1) == 0)
    def _():
        state_ref[...] = jnp.zeros_like(state_ref)

    def chunked(t):
        if nc == 1:
            return t
        return jnp.concatenate([t[:, c * chunk:(c + 1) * chunk] for c in range(nc)], axis=0)

    q = chunked(q_ref[...].astype(F32))
    k = chunked(k_ref[...].astype(F32))
    v = chunked(v_ref[...].astype(F32))

    gb = gb_ref[...]
    gbt = gbt_ref[...]
    gcc = chunked(jnp.stack([gb[:, h:h + 1] for h in range(H)]))
    bc = chunked(jnp.stack([gb[:, H + h:H + h + 1] for h in range(H)]))
    grc = jnp.stack([gbt[h:h + 1, c * chunk:(c + 1) * chunk]
                     for c in range(nc) for h in range(H)])
    g_last = gcc[:, chunk - 1:chunk, :]

    ri = lax.broadcasted_iota(jnp.int32, (1, chunk, chunk), 1)
    ci = lax.broadcasted_iota(jnp.int32, (1, chunk, chunk), 2)
    eg = jnp.exp(gcc)
    decay = jnp.exp(jnp.where(ri >= ci, gcc - grc, -jnp.inf))
    kb = k * bc
    kk_qk = jnp.einsum("bcd,bsd->bcs", jnp.concatenate([kb, q], axis=1).astype(BF16), k.astype(BF16),
                       preferred_element_type=F32)
    lmat = jnp.where(ri > ci, kk_qk[:, :chunk] * decay, 0.0)
    attn = (kk_qk[:, chunk:] * decay).astype(BF16)
    tinv = _unit_lower_inverse(lmat, ri, ci)
    uw = _bmm(tinv, jnp.concatenate([v * bc, kb * eg], axis=2))
    u, w = uw[:, :, :LANES], uw[:, :, LANES:]
    wq = jnp.concatenate([w, q * eg], axis=1).astype(BF16)
    kg = k * jnp.exp(g_last - gcc)
    kgt = jnp.stack([kg[b].T for b in range(nc * H)]).astype(BF16)
    egl = jnp.exp(g_last)

    state = state_ref[...]
    for c in range(nc):
        sel = slice(c * H, (c + 1) * H)
        ws_qs = _bmm(wq[sel], state)
        v_new = (u[sel] - ws_qs[:, :chunk]).astype(BF16)
        o = ws_qs[:, chunk:] + _bmm(attn[sel], v_new)
        state = state * egl[sel] + _bmm(kgt[sel], v_new)
        o_ref[:, c * chunk:(c + 1) * chunk, :] = o.astype(BF16)
    state_ref[...] = state


def gdn_core(p, gb, gbt, rows):
    B, _, S, _ = p.shape
    H = GDN_HEADS

    def slabs(group):
        return pl.BlockSpec((None, H, rows, LANES), lambda b, s: (b, group, s, 0))

    return pl.pallas_call(
        functools.partial(_gdn_core_kernel, rows=rows, chunk=GDN_CHUNK),
        grid=(B, S // rows),
        in_specs=[slabs(0), slabs(1), slabs(2),
                  pl.BlockSpec((None, rows, LANES), lambda b, s: (b, s, 0)),
                  pl.BlockSpec((None, 2 * H, rows), lambda b, s: (b, 0, s))],
        out_specs=pl.BlockSpec((None, H, rows, LANES), lambda b, s: (b, 0, s, 0)),
        out_shape=jax.ShapeDtypeStruct((B, H, S, LANES), BF16),
        scratch_shapes=[pltpu.VMEM((H, GDN_DK, LANES), F32)],
        compiler_params=_cparams(("arbitrary", "arbitrary"), 32 << 20),
        name="gdn_core",
    )(p, p, p, gb, gbt)


def _alibi_slope(head):
    return 2.0 ** (-8.0 * (head + 1) / SWA_Q_HEADS)


def _swa_kernel(sink_ref, qt_ref, kp_ref, kc_ref, vtp_ref, vtc_ref, o_ref, bias_ref, *, layer_slot):
    blk = pl.program_id(1)
    W = SWA_WINDOW
    hd = SWA_HEAD_DIM
    KV = SWA_KV_HEADS
    G = SWA_Q_HEADS // KV
    NEG = -jnp.inf

    @pl.when(jnp.logical_and(pl.program_id(0) == 0, blk == 0))
    def _():
        key = lax.broadcasted_iota(jnp.int32, (2 * W, G * W), 0)
        col = lax.broadcasted_iota(jnp.int32, (2 * W, G * W), 1)
        dist = (col & (W - 1)) + W - key
        valid = (dist >= 0) & (dist < SWA_WINDOW)
        distf = dist.astype(F32)
        for j in range(KV):
            slope = jnp.zeros((2 * W, G * W), F32)
            for a in range(G):
                slope = jnp.where((col >= a * W) & (col < (a + 1) * W), _alibi_slope(G * j + a), slope)
            bias = jnp.where(valid, -slope * distf, NEG)
            bias_ref[1, j] = bias
            bias_ref[0, j] = jnp.where(key >= W, bias, NEG)

    kcat = jnp.concatenate([kp_ref[...], kc_ref[...]], axis=0)
    vtcat = jnp.concatenate([vtp_ref[...], vtc_ref[...]], axis=1)
    zpad = jnp.zeros((hd, W), BF16)
    ks, ws, vts = [], [], []
    for j in range(KV):
        tile, half = divmod(j, LANES // hd)
        ks.append(kcat[:, tile * LANES:(tile + 1) * LANES])
        cols = []
        for a in range(G):
            h = G * j + a
            qt = qt_ref[h * hd:(h + 1) * hd, :]
            cols.append(jnp.concatenate([qt, zpad] if half == 0 else [zpad, qt], axis=0))
        ws.append(jnp.concatenate(cols, axis=1))
        vts.append(vtcat[j * hd:(j + 1) * hd, :])
    ks, ws, vts = jnp.stack(ks), jnp.stack(ws), jnp.stack(vts)

    s = jnp.einsum("jkd,jdq->jkq", ks, ws, preferred_element_type=F32)
    s = s + bias_ref[jnp.minimum(blk, 1)]
    lane = lax.broadcasted_iota(jnp.int32, (1, G * W), 1)
    sinks = []
    for j in range(KV):
        row = jnp.zeros((1, G * W), F32)
        for a in range(G):
            row = jnp.where((lane >= a * W) & (lane < (a + 1) * W), sink_ref[layer_slot, G * j + a], row)
        sinks.append(row)
    sink = jnp.stack(sinks)
    m = jnp.maximum(jnp.max(s, axis=1, keepdims=True), sink)
    p = jnp.exp(s - m)
    denom = jnp.sum(p, axis=1, keepdims=True) + jnp.exp(sink - m)
    ot = jnp.einsum("jdk,jkq->jdq", vts, p.astype(BF16), preferred_element_type=F32)
    ot = ot * (1.0 / denom)
    for t in range(SWA_Q_HEADS // 2):
        j, a0 = divmod(2 * t, G)
        pair = jnp.concatenate([ot[j][:, a0 * W:(a0 + 1) * W], ot[j][:, (a0 + 1) * W:(a0 + 2) * W]], axis=0)
        o_ref[:, t * LANES:(t + 1) * LANES] = pair.T.astype(BF16)


def swa_attention(qt, k_sh, vt_sh, sinks, layer_slot):
    B, QW, S = qt.shape
    KW = k_sh.shape[-1]
    W = SWA_WINDOW
    G = SWA_Q_HEADS // SWA_KV_HEADS
    prev = lambda n: jnp.maximum(n - 1, 0)
    return pl.pallas_call(
        functools.partial(_swa_kernel, layer_slot=layer_slot),
        grid=(B, S // W),
        in_specs=[pl.BlockSpec(memory_space=pltpu.SMEM),
                  pl.BlockSpec((None, QW, W), lambda b, n: (b, 0, n)),
                  pl.BlockSpec((None, W, KW), lambda b, n: (b, prev(n), 0)),
                  pl.BlockSpec((None, W, KW), lambda b, n: (b, n, 0)),
                  pl.BlockSpec((None, KW, W), lambda b, n: (b, 0, prev(n))),
                  pl.BlockSpec((None, KW, W), lambda b, n: (b, 0, n))],
        out_specs=pl.BlockSpec((None, W, QW), lambda b, n: (b, n, 0)),
        out_shape=jax.ShapeDtypeStruct((B, S, QW), BF16),
        scratch_shapes=[pltpu.VMEM((2, SWA_KV_HEADS, 2 * W, G * W), F32)],
        compiler_params=_cparams(("arbitrary", "arbitrary"), 32 << 20),
        name="swa_attention",
    )(sinks, qt, k_sh, k_sh, vt_sh, vt_sh)


def _post_mlp_kernel(*refs, gated, tail, ff_chunk, gdn_chunk):
    n_fixed = 12 if gated else 10
    if gated:
        (x_ref, a_ref, z_ref, onorm_ref, wo_ref, w1_ref, w2_ref, gains_ref,
         gate_mix_ref, shift_mlp_ref, scale_mlp_ref, gate_mlp_ref) = refs[:n_fixed]
    else:
        (x_ref, a_ref, wo_ref, w1_ref, w2_ref, gains_ref,
         gate_mix_ref, shift_mlp_ref, scale_mlp_ref, gate_mlp_ref) = refs[:n_fixed]
    rest = refs[n_fixed:]
    x = x_ref[...]
    if gated:
        g = (_rms(a_ref[...].astype(F32)) * onorm_ref[...] * _silu(z_ref[...].astype(F32))).astype(BF16)
        act = jnp.concatenate([g[h] for h in range(g.shape[0])], axis=1)
    else:
        act = a_ref[...]
    y = jnp.dot(act, wo_ref[...], preferred_element_type=F32)
    x = x + (1.0 + gate_mix_ref[...]) * (_rms(y) * gains_ref[1:2, :])
    h = _adaln_pre(x, gains_ref[2:3, :], shift_mlp_ref[...], scale_mlp_ref[...]).astype(BF16)
    d_ff = w1_ref.shape[1]
    acc = None
    for f0 in range(0, d_ff, ff_chunk):
        a = jnp.dot(h, w1_ref[:, f0:f0 + ff_chunk], preferred_element_type=F32)
        a = jnp.square(jnp.maximum(a, 0.0)).astype(BF16)
        part = jnp.dot(a, w2_ref[f0:f0 + ff_chunk, :], preferred_element_type=F32)
        acc = part if acc is None else acc + part
    x = x + (1.0 + gate_mlp_ref[...]) * (_rms(acc) * gains_ref[3:4, :])

    def q_proj(ngains_ref, nshift_ref, nscale_ref, wq_ref, q_ref):
        hq = _adaln_pre(x, ngains_ref[0:1, :], nshift_ref[...], nscale_ref[...]).astype(BF16)
        q = jnp.dot(hq, wq_ref[...], preferred_element_type=F32) * (SWA_HEAD_DIM ** -0.5)
        q_ref[...] = q.T.astype(BF16)

    if tail == "none":
        (xo_ref,) = rest
        xo_ref[...] = x
    elif tail == "gdn_in":
        (ngains_ref, nshift_ref, nscale_ref, w_ref, wab_ref, alog_ref, dtb_ref, convw_ref,
         xo_ref, p_ref, gb_ref, gbt_ref, stage_ref, carry_ref) = rest
        xo_ref[...] = x
        _gdn_in_body(x, pl.program_id(1) == 0, ngains_ref[0:1, :], nshift_ref, nscale_ref, w_ref, wab_ref,
                     alog_ref, dtb_ref, convw_ref, p_ref, gb_ref, gbt_ref, stage_ref, carry_ref, gdn_chunk)
    elif tail == "kv_q":
        (kvgain_ref, kvshift_ref, kvscale_ref, wkv_ref, ngains_ref, nshift_ref, nscale_ref, wq_ref,
         xo_ref, k_ref, v_ref, q_ref) = rest
        xo_ref[...] = x
        hk = _adaln_pre(x, kvgain_ref[...], kvshift_ref[...], kvscale_ref[...]).astype(BF16)
        kv = jnp.dot(hk, wkv_ref[...], preferred_element_type=F32)
        kvw = kv.shape[1] // 2
        k_ref[...] = kv[:, :kvw].astype(BF16)
        v_ref[...] = kv[:, kvw:].T.astype(BF16)
        q_proj(ngains_ref, nshift_ref, nscale_ref, wq_ref, q_ref)
    elif tail == "q":
        (ngains_ref, nshift_ref, nscale_ref, wq_ref, xo_ref, q_ref) = rest
        xo_ref[...] = x
        q_proj(ngains_ref, nshift_ref, nscale_ref, wq_ref, q_ref)
    else:
        raise ValueError(tail)


def post_mlp(x, act, mods, layer, gains, w_o, wo_idx, w1, w2, tm, *, gate_args=None,
             tail="none", tail_args=None):
    B, S, D = x.shape
    gated = gate_args is not None
    row = lambda width: pl.BlockSpec((None, tm, width), lambda b, s: (b, s, 0))
    col = lambda width: pl.BlockSpec((None, width, tm), lambda b, s: (b, 0, s))
    ins, in_specs = [x], [row(D)]
    if gated:
        p, onorm, oidx = gate_args
        H = act.shape[1]
        ins += [act, p, onorm]
        in_specs += [pl.BlockSpec((None, H, tm, LANES), lambda b, s: (b, 0, s, 0)),
                     pl.BlockSpec((None, H, tm, LANES), lambda b, s: (b, 3, s, 0)),
                     _layer_resident(onorm, oidx)]
    else:
        ins += [act]
        in_specs += [row(act.shape[-1])]
    ins += [w_o, w1, w2, gains, mods, mods, mods, mods]
    in_specs += [_layer_resident(w_o, wo_idx), _layer_resident(w1, layer), _layer_resident(w2, layer),
                 _layer_resident(gains, layer),
                 _mod_spec(D, layer, 2), _mod_spec(D, layer, 3), _mod_spec(D, layer, 4), _mod_spec(D, layer, 5)]
    out_specs = [row(D)]
    out_shape = [jax.ShapeDtypeStruct((B, S, D), F32)]
    scratch = []
    weight_bytes = (w_o[0].size + w1[0].size + w2[0].size) * 2
    extra = 0
    if tail == "gdn_in":
        nlayer, gp, gi = tail_args
        w_ins, w_specs, wb = _gdn_in_inputs(gp, gi)
        ins += [gains, mods, mods] + w_ins
        in_specs += [_layer_resident(gains, nlayer), _mod_spec(D, nlayer, 0), _mod_spec(D, nlayer, 1)] + w_specs
        n_slabs = gp["main"] // LANES
        os_, osh = _gdn_in_outputs(B, S, tm, n_slabs)
        out_specs += os_
        out_shape += osh
        scratch = _gdn_in_scratch(tm)
        weight_bytes += wb
        extra = 2 * tm * n_slabs * LANES * 2 + _gdn_in_scratch_bytes(tm)
    elif tail == "kv_q":
        kvmods, kvgain, w_kv, nlayer, w_q, qi = tail_args
        kv_spec = lambda k: pl.BlockSpec((None, None, 1, D), lambda b, s: (0, b, 0, k))
        ins += [kvgain.reshape(1, D), kvmods, kvmods, w_kv, gains, mods, mods, w_q]
        in_specs += [_resident((1, D)), kv_spec(0), kv_spec(1), _resident(w_kv.shape),
                     _layer_resident(gains, nlayer), _mod_spec(D, nlayer, 0), _mod_spec(D, nlayer, 1),
                     _layer_resident(w_q, qi)]
        kvw = w_kv.shape[1] // 2
        qw = w_q.shape[-1]
        out_specs += [row(kvw), col(kvw), col(qw)]
        out_shape += [jax.ShapeDtypeStruct((B, S, kvw), BF16), jax.ShapeDtypeStruct((B, kvw, S), BF16),
                      jax.ShapeDtypeStruct((B, qw, S), BF16)]
        weight_bytes += (w_kv.size + w_q[0].size) * 2
        extra = 2 * tm * (2 * kvw + qw) * 2
    elif tail == "q":
        nlayer, w_q, qi = tail_args
        qw = w_q.shape[-1]
        ins += [gains, mods, mods, w_q]
        in_specs += [_layer_resident(gains, nlayer), _mod_spec(D, nlayer, 0), _mod_spec(D, nlayer, 1),
                     _layer_resident(w_q, qi)]
        out_specs += [col(qw)]
        out_shape += [jax.ShapeDtypeStruct((B, qw, S), BF16)]
        weight_bytes += w_q[0].size * 2
        extra = 2 * tm * qw * 2
    ff_chunk = 512
    vmem = (weight_bytes + extra + 4 * tm * D * 4 + 4 * tm * D * 2 + 6 * tm * D * 4
            + 3 * tm * ff_chunk * 4 + (4 << 20))
    return pl.pallas_call(
        functools.partial(_post_mlp_kernel, gated=gated, tail=tail, ff_chunk=ff_chunk, gdn_chunk=GDN_CHUNK),
        grid=(B, S // tm),
        in_specs=in_specs, out_specs=out_specs, out_shape=out_shape,
        scratch_shapes=scratch,
        compiler_params=_cparams(("arbitrary", "arbitrary"), vmem),
        name="post_mlp_" + tail,
    )(*ins)


def kernel(x, c, mod_w, mod_b, norm_g, gdn_w_in, gdn_conv, gdn_a_log, gdn_dt_bias, gdn_onorm,
           gdn_w_out, kv_mod_w, kv_mod_b, kv_norm, w_kv, swa_w_q, swa_sinks, swa_w_o,
           mlp_w1, mlp_w2):
    B, S, D = x.shape
    depth = mod_w.shape[0]
    n_gdn = gdn_w_in.shape[0]
    tm = min(S, 512)
    rows = min(S, 128)
    assert S % tm == 0 and S % rows == 0 and rows % GDN_CHUNK == 0 and S % SWA_WINDOW == 0

    mods = modulation(c, mod_w, mod_b).reshape(depth, B, 1, mod_w.shape[-1])
    kvmods = modulation(c, kv_mod_w[None], kv_mod_b[None]).reshape(1, B, 1, kv_mod_w.shape[-1])

    w1 = mlp_w1.astype(BF16)
    w2 = mlp_w2.astype(BF16)
    gp = _gdn_params(gdn_w_in, gdn_conv, gdn_a_log, gdn_dt_bias)
    gdn_wo = gdn_w_out.astype(BF16)
    swa_wq = swa_w_q.astype(BF16)
    swa_wo = swa_w_o.astype(BF16)
    wkv = w_kv.astype(BF16)
    sinks = swa_sinks.astype(F32)
    gains = norm_g.astype(F32)
    onorm = gdn_onorm.astype(F32).reshape(n_gdn, 1, LANES)

    p, gb, gbt = gdn_in_proj(x, mods, 0, gains, gp, 0, tm)
    k_sh = v_sh = q = None
    for layer in range(depth):
        if layer < n_gdn:
            act = gdn_core(p, gb, gbt, rows)
            w_o, wo_idx, gate_args = gdn_wo, layer, (p, onorm, layer)
        else:
            act = swa_attention(q, k_sh, v_sh, sinks, layer - n_gdn)
            w_o, wo_idx, gate_args = swa_wo, layer - n_gdn, None
        nxt = layer + 1
        if nxt == depth:
            tail, tail_args = "none", None
        elif nxt < n_gdn:
            tail, tail_args = "gdn_in", (nxt, gp, nxt)
        elif nxt == n_gdn:
            tail, tail_args = "kv_q", (kvmods, kv_norm, wkv, nxt, swa_wq, 0)
        else:
            tail, tail_args = "q", (nxt, swa_wq, nxt - n_gdn)
        outs = post_mlp(x, act, mods, layer, gains, w_o, wo_idx, w1, w2, tm,
                        gate_args=gate_args, tail=tail, tail_args=tail_args)
        x = outs[0]
        if tail == "gdn_in":
            p, gb, gbt = outs[1:]
        elif tail == "kv_q":
            k_sh, v_sh, q = outs[1:]
        elif tail == "q":
            (q,) = outs[1:]
    return x
```

```python
import functools

import jax
import jax.numpy as jnp
from jax import lax
from jax.experimental import pallas as pl
from jax.experimental.pallas import tpu as pltpu

F32 = jnp.float32
BF16 = jnp.bfloat16

NORM_EPS = 1e-6
LANES = 128
SUBLANES = 8
GDN_HEADS = 8
GDN_DK = 128
GDN_CONV = 4
SWA_HEAD_DIM = 64
SWA_Q_HEADS = 16
SWA_KV_HEADS = 4
SWA_WINDOW = 128
V7X_SCOPED_VMEM_BYTES = 60000 * 1024

GDN_CHUNK = 128
GDN_SLAB_GROUP = 2
CONV_HALO = SUBLANES


def _cparams(semantics, vmem_bytes):
    return pltpu.CompilerParams(dimension_semantics=semantics,
                                vmem_limit_bytes=min(int(vmem_bytes), V7X_SCOPED_VMEM_BYTES))


def _sigmoid(x):
    return 1.0 / (1.0 + jnp.exp(-x))


def _silu(x):
    return x * _sigmoid(x)


def _rms(x):
    return x * lax.rsqrt(jnp.mean(x * x, axis=-1, keepdims=True) + NORM_EPS)


def _bdot(a, b):
    return jnp.dot(a.astype(BF16), b.astype(BF16), preferred_element_type=F32)


def _bmm(a, b):
    return jnp.einsum("bij,bjk->bik", a.astype(BF16), b.astype(BF16), preferred_element_type=F32)


def _resident(shape):
    nd = len(shape)
    return pl.BlockSpec(shape, lambda *_: (0,) * nd, pipeline_mode=pl.Buffered(1))


def _layer_resident(arr, layer, block=None):
    block = tuple(arr.shape[1:]) if block is None else tuple(block)
    nd = len(block)
    return pl.BlockSpec((None,) + block, lambda *_: (layer,) + (0,) * nd, pipeline_mode=pl.Buffered(1))


def _mod_kernel(c_ref, w_ref, b_ref, o_ref):
    o_ref[0] = _bdot(_silu(c_ref[...]), w_ref[0]) + b_ref[0]


def modulation(c, w, b):
    L, D, N = w.shape
    B = c.shape[0]
    tn = min(N, 1024)
    assert N % tn == 0
    return pl.pallas_call(
        _mod_kernel,
        grid=(L, N // tn),
        in_specs=[pl.BlockSpec((B, D), lambda l, n: (0, 0)),
                  pl.BlockSpec((1, D, tn), lambda l, n: (l, 0, n)),
                  pl.BlockSpec((1, 1, tn), lambda l, n: (l, 0, n))],
        out_specs=pl.BlockSpec((1, B, tn), lambda l, n: (l, 0, n)),
        out_shape=jax.ShapeDtypeStruct((L, B, N), F32),
        compiler_params=_cparams(("arbitrary", "arbitrary"), 4 * D * tn * 4 + (4 << 20)),
        name="modulation",
    )(c, w, b.reshape(L, 1, N))


def _mod_spec(D, layer, k):
    return pl.BlockSpec((None, None, 1, D), lambda b, *_: (layer, b, 0, k))


def _adaln_pre(x, gain, shift, scale):
    return _rms(x) * gain * (1.0 + scale) + shift


def _gdn_gates(ab, alog_ref, dtb_ref, out_gb_ref, out_gbt_ref, chunk):
    tm = ab.shape[0]
    lane = lax.broadcasted_iota(jnp.int32, ab.shape, 1)
    pre = ab + dtb_ref[...]
    softplus = jnp.maximum(pre, 0.0) + jnp.log(1.0 + jnp.exp(-jnp.abs(pre)))
    g = -jnp.exp(alog_ref[...]) * softplus
    g = jnp.where(lane < GDN_HEADS, g, 0.0)
    beta = _sigmoid(ab)
    r = lax.broadcasted_iota(jnp.int32, (chunk, chunk), 0)
    c = lax.broadcasted_iota(jnp.int32, (chunk, chunk), 1)
    tri = (r >= c).astype(F32)
    parts = []
    for i in range(tm // chunk):
        parts.append(jnp.dot(tri, g[i * chunk:(i + 1) * chunk], preferred_element_type=F32,
                             precision=lax.Precision.HIGHEST))
    gcum = jnp.concatenate(parts, axis=0) if len(parts) > 1 else parts[0]
    gb = jnp.where(lane < GDN_HEADS, gcum, jnp.where(lane < 2 * GDN_HEADS, beta, 0.0))
    out_gb_ref[...] = gb
    out_gbt_ref[...] = gb.T[:2 * GDN_HEADS]


def _gdn_in_body(x, first_row_block, gain, shift_ref, scale_ref, w_ref, wab_ref, alog_ref, dtb_ref,
                 convw_ref, p_ref, gb_ref, gbt_ref, stage_ref, carry_ref, chunk):
    tm = x.shape[0]
    group = GDN_SLAB_GROUP
    H = GDN_HEADS

    @pl.when(first_row_block)
    def _():
        carry_ref[...] = jnp.zeros_like(carry_ref)

    h = _adaln_pre(x, gain, shift_ref[...], scale_ref[...]).astype(BF16)
    n_slabs = w_ref.shape[1] // LANES

    def epilogue(gi, res):
        g0 = gi * group
        if g0 >= 3 * H:
            for j in range(group):
                p_ref[g0 + j] = res[:, j * LANES:(j + 1) * LANES].astype(BF16)
            return
        buf = gi % 2
        stage_ref[buf, :, 0:CONV_HALO, :] = carry_ref[g0:g0 + group]
        for j in range(group):
            stage_ref[buf, j, CONV_HALO:CONV_HALO + tm, :] = res[:, j * LANES:(j + 1) * LANES]
        acc = None
        for t in range(GDN_CONV):
            off = CONV_HALO - (GDN_CONV - 1) + t
            term = convw_ref[t, g0:g0 + group] * stage_ref[buf, :, off:off + tm, :]
            acc = term if acc is None else acc + term
        carry_ref[g0:g0 + group] = stage_ref[buf, :, tm:tm + CONV_HALO, :]
        y = _silu(acc)
        if g0 < 2 * H:
            y = y * lax.rsqrt(jnp.sum(y * y, axis=-1, keepdims=True) + NORM_EPS)
        if g0 < H:
            y = y * (GDN_DK ** -0.5)
        p_ref[g0:g0 + group] = y.astype(BF16)

    pending = None
    for gi in range(n_slabs // group):
        g0 = gi * group
        res = jnp.dot(h, w_ref[:, g0 * LANES:(g0 + group) * LANES], preferred_element_type=F32)
        if pending is not None:
            epilogue(*pending)
        pending = (gi, res)
    ab = jnp.dot(h, wab_ref[...], preferred_element_type=F32)
    epilogue(*pending)
    _gdn_gates(ab, alog_ref, dtb_ref, gb_ref, gbt_ref, chunk)


def _gdn_in_kernel(x_ref, gains_ref, shift_ref, scale_ref, w_ref, wab_ref, alog_ref, dtb_ref, convw_ref,
                   p_ref, gb_ref, gbt_ref, stage_ref, carry_ref, *, chunk):
    _gdn_in_body(x_ref[...], pl.program_id(1) == 0, gains_ref[0:1, :], shift_ref, scale_ref, w_ref, wab_ref,
                 alog_ref, dtb_ref, convw_ref, p_ref, gb_ref, gbt_ref, stage_ref, carry_ref, chunk)


def _gdn_in_outputs(B, S, tm, n_slabs):
    out_specs = [pl.BlockSpec((None, n_slabs, tm, LANES), lambda b, s: (b, 0, s, 0)),
                 pl.BlockSpec((None, tm, LANES), lambda b, s: (b, s, 0)),
                 pl.BlockSpec((None, 2 * GDN_HEADS, tm), lambda b, s: (b, 0, s))]
    out_shape = [jax.ShapeDtypeStruct((B, n_slabs, S, LANES), BF16),
                 jax.ShapeDtypeStruct((B, S, LANES), F32),
                 jax.ShapeDtypeStruct((B, 2 * GDN_HEADS, S), F32)]
    return out_specs, out_shape


def _gdn_in_scratch(tm):
    return [pltpu.VMEM((2, GDN_SLAB_GROUP, tm + CONV_HALO, LANES), F32),
            pltpu.VMEM((3 * GDN_HEADS, CONV_HALO, LANES), F32)]


def _gdn_in_scratch_bytes(tm):
    return 2 * GDN_SLAB_GROUP * (tm + CONV_HALO) * LANES * 4 + 3 * GDN_HEADS * CONV_HALO * LANES * 4


def _gdn_params(w_in, conv_w, a_log, dt_bias):
    n, D, n_in = w_in.shape
    main = n_in - 2 * GDN_HEADS
    w_all = w_in.astype(BF16)
    w_ab = jnp.zeros((n, D, LANES), F32).at[:, :, :2 * GDN_HEADS].set(w_in[:, :, main:]).astype(BF16)
    alog = jnp.zeros((n, 1, LANES), F32).at[:, 0, :GDN_HEADS].set(a_log.astype(F32))
    dtb = jnp.zeros((n, 1, LANES), F32).at[:, 0, :GDN_HEADS].set(dt_bias.astype(F32))
    convw = conv_w.astype(F32).reshape(n, GDN_CONV, main // LANES - GDN_HEADS, 1, LANES)
    return dict(w_all=w_all, main=main, w_ab=w_ab, alog=alog, dtb=dtb, convw=convw)


def _gdn_in_inputs(gp, i):
    D = gp["w_all"].shape[1]
    ins = [gp["w_all"], gp["w_ab"], gp["alog"], gp["dtb"], gp["convw"]]
    specs = [_layer_resident(gp["w_all"], i, (D, gp["main"])), _layer_resident(gp["w_ab"], i),
             _layer_resident(gp["alog"], i), _layer_resident(gp["dtb"], i), _layer_resident(gp["convw"], i)]
    weight_bytes = (D * gp["main"] + D * LANES) * 2 + gp["convw"][0].size * 4 * SUBLANES
    return ins, specs, weight_bytes


def gdn_in_proj(x, mods, layer, gains, gp, i, tm):
    B, S, D = x.shape
    n_slabs = gp["main"] // LANES
    out_specs, out_shape = _gdn_in_outputs(B, S, tm, n_slabs)
    w_ins, w_specs, weight_bytes = _gdn_in_inputs(gp, i)
    vmem = (2 * tm * D * 4 + weight_bytes + 2 * tm * n_slabs * LANES * 2 + _gdn_in_scratch_bytes(tm)
            + 6 * tm * GDN_SLAB_GROUP * LANES * 4 + (6 << 20))
    return pl.pallas_call(
        functools.partial(_gdn_in_kernel, chunk=GDN_CHUNK),
        grid=(B, S // tm),
        in_specs=[pl.BlockSpec((None, tm, D), lambda b, s: (b, s, 0)),
                  _layer_resident(gains, layer),
                  _mod_spec(D, layer, 0), _mod_spec(D, layer, 1)] + w_specs,
        out_specs=out_specs, out_shape=out_shape,
        scratch_shapes=_gdn_in_scratch(tm),
        compiler_params=_cparams(("arbitrary", "arbitrary"), vmem),
        name="gdn_in_proj",
    )(x, gains, mods, mods, *w_ins)


def _unit_lower_inverse(lmat, ri, ci):
    n = lmat.shape[-1]
    diff = ri ^ ci
    p = jnp.where(ri == ci, 1.0, 0.0) - jnp.where(diff < 2, lmat, 0.0)
    s = 2
    while s < n:
        off = jnp.where(diff >= s, jnp.where(diff < 2 * s, lmat, 0.0), 0.0)
        p = p - _bmm(p, _bmm(off, p))
        s *= 2
    return p


def _gdn_core_kernel(q_ref, k_ref, v_ref, gb_ref, gbt_ref, o_ref, state_ref, *, rows, chunk):
    H = GDN_HEADS
    nc = rows // chunk

    @pl.when(pl.program_id(1) == 0)
    def _():
        state_ref[...] = jnp.zeros_like(state_ref)

    def chunked(t):
        if nc == 1:
            return t
        return jnp.concatenate([t[:, c * chunk:(c + 1) * chunk] for c in range(nc)], axis=0)

    q = chunked(q_ref[...])
    k = chunked(k_ref[...])
    v = chunked(v_ref[...])

    gb = gb_ref[...]
    gbt = gbt_ref[...]
    gcc = chunked(jnp.stack([gb[:, h:h + 1] for h in range(H)]))
    bc = chunked(jnp.stack([gb[:, H + h:H + h + 1] for h in range(H)]))

    def rows_of(base):
        return jnp.stack([gbt[base + h:base + h + 1, c * chunk:(c + 1) * chunk]
                          for c in range(nc) for h in range(H)])

    grc, brc = rows_of(0), rows_of(H)
    g_last = gcc[:, chunk - 1:chunk, :]

    ri = lax.broadcasted_iota(jnp.int32, (1, chunk, chunk), 1)
    ci = lax.broadcasted_iota(jnp.int32, (1, chunk, chunk), 2)
    gcb = jnp.broadcast_to(gcc, (nc * H, chunk, LANES))
    decay = jnp.exp(jnp.where(ri >= ci, gcb[:, :, :chunk] - grc, -jnp.inf))
    kk_qk = jnp.einsum("bcd,bsd->bcs", jnp.concatenate([k, q], axis=1), k,
                       preferred_element_type=F32)
    lmat = jnp.where(ri > ci, kk_qk[:, :chunk] * decay * bc, 0.0)
    attn = (kk_qk[:, chunk:] * decay).astype(BF16)
    tinv = _unit_lower_inverse(lmat, ri, ci)
    u = _bmm(tinv * brc, v)
    w = _bmm(tinv * (brc * jnp.exp(grc)), k)
    wq = jnp.concatenate([w.astype(BF16), q], axis=1)
    eg = jnp.exp(gcb)
    kt = jnp.stack([k[b].astype(F32).T for b in range(nc * H)])
    kgt = (kt * jnp.exp(g_last - grc)).astype(BF16)
    egl = jnp.exp(g_last)

    state = state_ref[...]
    for c in range(nc):
        sel = slice(c * H, (c + 1) * H)
        ws_qs = _bmm(wq[sel], state)
        v_new = (u[sel] - ws_qs[:, :chunk]).astype(BF16)
        o = ws_qs[:, chunk:] * eg[sel] + _bmm(attn[sel], v_new)
        state = state * egl[sel] + _bmm(kgt[sel], v_new)
        o_ref[:, c * chunk:(c + 1) * chunk, :] = o.astype(BF16)
    state_ref[...] = state


def gdn_core(p, gb, gbt, rows):
    B, _, S, _ = p.shape
    H = GDN_HEADS

    def slabs(group):
        return pl.BlockSpec((None, H, rows, LANES), lambda b, s: (b, group, s, 0))

    return pl.pallas_call(
        functools.partial(_gdn_core_kernel, rows=rows, chunk=GDN_CHUNK),
        grid=(B, S // rows),
        in_specs=[slabs(0), slabs(1), slabs(2),
                  pl.BlockSpec((None, rows, LANES), lambda b, s: (b, s, 0)),
                  pl.BlockSpec((None, 2 * H, rows), lambda b, s: (b, 0, s))],
        out_specs=pl.BlockSpec((None, H, rows, LANES), lambda b, s: (b, 0, s, 0)),
        out_shape=jax.ShapeDtypeStruct((B, H, S, LANES), BF16),
        scratch_shapes=[pltpu.VMEM((H, GDN_DK, LANES), F32)],
        compiler_params=_cparams(("arbitrary", "arbitrary"), 32 << 20),
        name="gdn_core",
    )(p, p, p, gb, gbt)


def _alibi_slope(head):
    return 2.0 ** (-8.0 * (head + 1) / SWA_Q_HEADS)


def _swa_kernel(sink_ref, qt_ref, kp_ref, kc_ref, vtp_ref, vtc_ref, o_ref, bias_ref, *, layer_slot):
    blk = pl.program_id(1)
    W = SWA_WINDOW
    hd = SWA_HEAD_DIM
    KV = SWA_KV_HEADS
    G = SWA_Q_HEADS // KV
    NEG = -jnp.inf

    @pl.when(jnp.logical_and(pl.program_id(0) == 0, blk == 0))
    def _():
        key = lax.broadcasted_iota(jnp.int32, (2 * W, G * W), 0)
        col = lax.broadcasted_iota(jnp.int32, (2 * W, G * W), 1)
        dist = (col & (W - 1)) + W - key
        valid = (dist >= 0) & (dist < SWA_WINDOW)
        distf = dist.astype(F32)
        for j in range(KV):
            slope = jnp.zeros((2 * W, G * W), F32)
            for a in range(G):
                slope = jnp.where((col >= a * W) & (col < (a + 1) * W), _alibi_slope(G * j + a), slope)
            bias = jnp.where(valid, -slope * distf, NEG)
            bias_ref[1, j] = bias
            bias_ref[0, j] = jnp.where(key >= W, bias, NEG)

    kcat = jnp.concatenate([kp_ref[...], kc_ref[...]], axis=0)
    vtcat = jnp.concatenate([vtp_ref[...], vtc_ref[...]], axis=1)
    zpad = jnp.zeros((hd, W), BF16)
    ks, ws, vts = [], [], []
    for j in range(KV):
        tile, half = divmod(j, LANES // hd)
        ks.append(kcat[:, tile * LANES:(tile + 1) * LANES])
        cols = []
        for a in range(G):
            h = G * j + a
            qt = qt_ref[h * hd:(h + 1) * hd, :]
            cols.append(jnp.concatenate([qt, zpad] if half == 0 else [zpad, qt], axis=0))
        ws.append(jnp.concatenate(cols, axis=1))
        vts.append(vtcat[j * hd:(j + 1) * hd, :])
    ks, ws, vts = jnp.stack(ks), jnp.stack(ws), jnp.stack(vts)

    s = jnp.einsum("jkd,jdq->jkq", ks, ws, preferred_element_type=F32)
    s = s + bias_ref[jnp.minimum(blk, 1)]
    lane = lax.broadcasted_iota(jnp.int32, (1, G * W), 1)
    sinks = []
    for j in range(KV):
        row = jnp.zeros((1, G * W), F32)
        for a in range(G):
            row = jnp.where((lane >= a * W) & (lane < (a + 1) * W), sink_ref[layer_slot, G * j + a], row)
        sinks.append(row)
    sink = jnp.stack(sinks)
    m = jnp.maximum(jnp.max(s, axis=1, keepdims=True), sink)
    p = jnp.exp(s - m)
    denom = jnp.sum(p, axis=1, keepdims=True) + jnp.exp(sink - m)
    ot = jnp.einsum("jdk,jkq->jdq", vts, p.astype(BF16), preferred_element_type=F32)
    ot = ot * (1.0 / denom)
    for t in range(SWA_Q_HEADS // 2):
        j, a0 = divmod(2 * t, G)
        pair = jnp.concatenate([ot[j][:, a0 * W:(a0 + 1) * W], ot[j][:, (a0 + 1) * W:(a0 + 2) * W]], axis=0)
        o_ref[:, t * LANES:(t + 1) * LANES] = pair.T.astype(BF16)


def swa_attention(qt, k_sh, vt_sh, sinks, layer_slot):
    B, QW, S = qt.shape
    KW = k_sh.shape[-1]
    W = SWA_WINDOW
    G = SWA_Q_HEADS // SWA_KV_HEADS
    prev = lambda n: jnp.maximum(n - 1, 0)
    return pl.pallas_call(
        functools.partial(_swa_kernel, layer_slot=layer_slot),
        grid=(B, S // W),
        in_specs=[pl.BlockSpec(memory_space=pltpu.SMEM),
                  pl.BlockSpec((None, QW, W), lambda b, n: (b, 0, n)),
                  pl.BlockSpec((None, W, KW), lambda b, n: (b, prev(n), 0)),
                  pl.BlockSpec((None, W, KW), lambda b, n: (b, n, 0)),
                  pl.BlockSpec((None, KW, W), lambda b, n: (b, 0, prev(n))),
                  pl.BlockSpec((None, KW, W), lambda b, n: (b, 0, n))],
        out_specs=pl.BlockSpec((None, W, QW), lambda b, n: (b, n, 0)),
        out_shape=jax.ShapeDtypeStruct((B, S, QW), BF16),
        scratch_shapes=[pltpu.VMEM((2, SWA_KV_HEADS, 2 * W, G * W), F32)],
        compiler_params=_cparams(("arbitrary", "arbitrary"), 32 << 20),
        name="swa_attention",
    )(sinks, qt, k_sh, k_sh, vt_sh, vt_sh)


def _post_mlp_kernel(*refs, gated, tail, ff_chunk, gdn_chunk):
    n_fixed = 12 if gated else 10
    if gated:
        (x_ref, a_ref, z_ref, onorm_ref, wo_ref, w1_ref, w2_ref, gains_ref,
         gate_mix_ref, shift_mlp_ref, scale_mlp_ref, gate_mlp_ref) = refs[:n_fixed]
    else:
        (x_ref, a_ref, wo_ref, w1_ref, w2_ref, gains_ref,
         gate_mix_ref, shift_mlp_ref, scale_mlp_ref, gate_mlp_ref) = refs[:n_fixed]
    rest = refs[n_fixed:]
    x = x_ref[...]
    if gated:
        g = (_rms(a_ref[...].astype(F32)) * onorm_ref[...] * _silu(z_ref[...].astype(F32))).astype(BF16)
        act = jnp.concatenate([g[h] for h in range(g.shape[0])], axis=1)
    else:
        act = a_ref[...]
    y = jnp.dot(act, wo_ref[...], preferred_element_type=F32)
    x = x + (1.0 + gate_mix_ref[...]) * (_rms(y) * gains_ref[1:2, :])
    h = _adaln_pre(x, gains_ref[2:3, :], shift_mlp_ref[...], scale_mlp_ref[...]).astype(BF16)
    d_ff = w1_ref.shape[1]
    acc = None
    for f0 in range(0, d_ff, ff_chunk):
        a = jnp.dot(h, w1_ref[:, f0:f0 + ff_chunk], preferred_element_type=F32)
        a = jnp.square(jnp.maximum(a, 0.0)).astype(BF16)
        part = jnp.dot(a, w2_ref[f0:f0 + ff_chunk, :], preferred_element_type=F32)
        acc = part if acc is None else acc + part
    x = x + (1.0 + gate_mlp_ref[...]) * (_rms(acc) * gains_ref[3:4, :])

    def q_proj(ngains_ref, nshift_ref, nscale_ref, wq_ref, q_ref):
        hq = _adaln_pre(x, ngains_ref[0:1, :], nshift_ref[...], nscale_ref[...]).astype(BF16)
        q = jnp.dot(hq, wq_ref[...], preferred_element_type=F32) * (SWA_HEAD_DIM ** -0.5)
        q_ref[...] = q.T.astype(BF16)

    if tail == "none":
        (xo_ref,) = rest
        xo_ref[...] = x
    elif tail == "gdn_in":
        (ngains_ref, nshift_ref, nscale_ref, w_ref, wab_ref, alog_ref, dtb_ref, convw_ref,
         xo_ref, p_ref, gb_ref, gbt_ref, stage_ref, carry_ref) = rest
        xo_ref[...] = x
        _gdn_in_body(x, pl.program_id(1) == 0, ngains_ref[0:1, :], nshift_ref, nscale_ref, w_ref, wab_ref,
                     alog_ref, dtb_ref, convw_ref, p_ref, gb_ref, gbt_ref, stage_ref, carry_ref, gdn_chunk)
    elif tail == "kv_q":
        (kvgain_ref, kvshift_ref, kvscale_ref, wkv_ref, ngains_ref, nshift_ref, nscale_ref, wq_ref,
         xo_ref, k_ref, v_ref, q_ref) = rest
        xo_ref[...] = x
        hk = _adaln_pre(x, kvgain_ref[...], kvshift_ref[...], kvscale_ref[...]).astype(BF16)
        kv = jnp.dot(hk, wkv_ref[...], preferred_element_type=F32)
        kvw = kv.shape[1] // 2
        k_ref[...] = kv[:, :kvw].astype(BF16)
        v_ref[...] = kv[:, kvw:].T.astype(BF16)
        q_proj(ngains_ref, nshift_ref, nscale_ref, wq_ref, q_ref)
    elif tail == "q":
        (ngains_ref, nshift_ref, nscale_ref, wq_ref, xo_ref, q_ref) = rest
        xo_ref[...] = x
        q_proj(ngains_ref, nshift_ref, nscale_ref, wq_ref, q_ref)
    else:
        raise ValueError(tail)


def post_mlp(x, act, mods, layer, gains, w_o, wo_idx, w1, w2, tm, *, gate_args=None,
             tail="none", tail_args=None):
    B, S, D = x.shape
    gated = gate_args is not None
    row = lambda width: pl.BlockSpec((None, tm, width), lambda b, s: (b, s, 0))
    col = lambda width: pl.BlockSpec((None, width, tm), lambda b, s: (b, 0, s))
    ins, in_specs = [x], [row(D)]
    if gated:
        p, onorm, oidx = gate_args
        H = act.shape[1]
        ins += [act, p, onorm]
        in_specs += [pl.BlockSpec((None, H, tm, LANES), lambda b, s: (b, 0, s, 0)),
                     pl.BlockSpec((None, H, tm, LANES), lambda b, s: (b, 3, s, 0)),
                     _layer_resident(onorm, oidx)]
    else:
        ins += [act]
        in_specs += [row(act.shape[-1])]
    ins += [w_o, w1, w2, gains, mods, mods, mods, mods]
    in_specs += [_layer_resident(w_o, wo_idx), _layer_resident(w1, layer), _layer_resident(w2, layer),
                 _layer_resident(gains, layer),
                 _mod_spec(D, layer, 2), _mod_spec(D, layer, 3), _mod_spec(D, layer, 4), _mod_spec(D, layer, 5)]
    out_specs = [row(D)]
    out_shape = [jax.ShapeDtypeStruct((B, S, D), F32)]
    scratch = []
    weight_bytes = (w_o[0].size + w1[0].size + w2[0].size) * 2
    extra = 0
    if tail == "gdn_in":
        nlayer, gp, gi = tail_args
        w_ins, w_specs, wb = _gdn_in_inputs(gp, gi)
        ins += [gains, mods, mods] + w_ins
        in_specs += [_layer_resident(gains, nlayer), _mod_spec(D, nlayer, 0), _mod_spec(D, nlayer, 1)] + w_specs
        n_slabs = gp["main"] // LANES
        os_, osh = _gdn_in_outputs(B, S, tm, n_slabs)
        out_specs += os_
        out_shape += osh
        scratch = _gdn_in_scratch(tm)
        weight_bytes += wb
        extra = 2 * tm * n_slabs * LANES * 2 + _gdn_in_scratch_bytes(tm)
    elif tail == "kv_q":
        kvmods, kvgain, w_kv, nlayer, w_q, qi = tail_args
        kv_spec = lambda k: pl.BlockSpec((None, None, 1, D), lambda b, s: (0, b, 0, k))
        ins += [kvgain.reshape(1, D), kvmods, kvmods, w_kv, gains, mods, mods, w_q]
        in_specs += [_resident((1, D)), kv_spec(0), kv_spec(1), _resident(w_kv.shape),
                     _layer_resident(gains, nlayer), _mod_spec(D, nlayer, 0), _mod_spec(D, nlayer, 1),
                     _layer_resident(w_q, qi)]
        kvw = w_kv.shape[1] // 2
        qw = w_q.shape[-1]
        out_specs += [row(kvw), col(kvw), col(qw)]
        out_shape += [jax.ShapeDtypeStruct((B, S, kvw), BF16), jax.ShapeDtypeStruct((B, kvw, S), BF16),
                      jax.ShapeDtypeStruct((B, qw, S), BF16)]
        weight_bytes += (w_kv.size + w_q[0].size) * 2
        extra = 2 * tm * (2 * kvw + qw) * 2
    elif tail == "q":
        nlayer, w_q, qi = tail_args
        qw = w_q.shape[-1]
        ins += [gains, mods, mods, w_q]
        in_specs += [_layer_resident(gains, nlayer), _mod_spec(D, nlayer, 0), _mod_spec(D, nlayer, 1),
                     _layer_resident(w_q, qi)]
        out_specs += [col(qw)]
        out_shape += [jax.ShapeDtypeStruct((B, qw, S), BF16)]
        weight_bytes += w_q[0].size * 2
        extra = 2 * tm * qw * 2
    ff_chunk = 512
    vmem = (weight_bytes + extra + 4 * tm * D * 4 + 4 * tm * D * 2 + 6 * tm * D * 4
            + 3 * tm * ff_chunk * 4 + (4 << 20))
    return pl.pallas_call(
        functools.partial(_post_mlp_kernel, gated=gated, tail=tail, ff_chunk=ff_chunk, gdn_chunk=GDN_CHUNK),
        grid=(B, S // tm),
        in_specs=in_specs, out_specs=out_specs, out_shape=out_shape,
        scratch_shapes=scratch,
        compiler_params=_cparams(("arbitrary", "arbitrary"), vmem),
        name="post_mlp_" + tail,
    )(*ins)


def kernel(x, c, mod_w, mod_b, norm_g, gdn_w_in, gdn_conv, gdn_a_log, gdn_dt_bias, gdn_onorm,
           gdn_w_out, kv_mod_w, kv_mod_b, kv_norm, w_kv, swa_w_q, swa_sinks, swa_w_o,
           mlp_w1, mlp_w2):
    B, S, D = x.shape
    depth = mod_w.shape[0]
    n_gdn = gdn_w_in.shape[0]
    tm = min(S, 512)
    rows = min(S, 256)
    assert S % tm == 0 and S % rows == 0 and rows % GDN_CHUNK == 0 and S % SWA_WINDOW == 0

    mods = modulation(c, mod_w, mod_b).reshape(depth, B, 1, mod_w.shape[-1])
    kvmods = modulation(c, kv_mod_w[None], kv_mod_b[None]).reshape(1, B, 1, kv_mod_w.shape[-1])

    w1 = mlp_w1.astype(BF16)
    w2 = mlp_w2.astype(BF16)
    gp = _gdn_params(gdn_w_in, gdn_conv, gdn_a_log, gdn_dt_bias)
    gdn_wo = gdn_w_out.astype(BF16)
    swa_wq = swa_w_q.astype(BF16)
    swa_wo = swa_w_o.astype(BF16)
    wkv = w_kv.astype(BF16)
    sinks = swa_sinks.astype(F32)
    gains = norm_g.astype(F32)
    onorm = gdn_onorm.astype(F32).reshape(n_gdn, 1, LANES)

    p, gb, gbt = gdn_in_proj(x, mods, 0, gains, gp, 0, tm)
    k_sh = v_sh = q = None
    for layer in range(depth):
        if layer < n_gdn:
            act = gdn_core(p, gb, gbt, rows)
            w_o, wo_idx, gate_args = gdn_wo, layer, (p, onorm, layer)
        else:
            act = swa_attention(q, k_sh, v_sh, sinks, layer - n_gdn)
            w_o, wo_idx, gate_args = swa_wo, layer - n_gdn, None
        nxt = layer + 1
        if nxt == depth:
            tail, tail_args = "none", None
        elif nxt < n_gdn:
            tail, tail_args = "gdn_in", (nxt, gp, nxt)
        elif nxt == n_gdn:
            tail, tail_args = "kv_q", (kvmods, kv_norm, wkv, nxt, swa_wq, 0)
        else:
            tail, tail_args = "q", (nxt, swa_wq, nxt - n_gdn)
        outs = post_mlp(x, act, mods, layer, gains, w_o, wo_idx, w1, w2, tm,
                        gate_args=gate_args, tail=tail, tail_args=tail_args)
        x = outs[0]
        if tail == "gdn_in":
            p, gb, gbt = outs[1:]
        elif tail == "kv_q":
            k_sh, v_sh, q = outs[1:]
        elif tail == "q":
            (q,) = outs[1:]
    return x
```

```python
import functools

import jax
import jax.numpy as jnp
from jax import lax
from jax.experimental import pallas as pl
from jax.experimental.pallas import tpu as pltpu

F32 = jnp.float32
BF16 = jnp.bfloat16

NORM_EPS = 1e-6
LOG2_E = 1.4426950408889634
LANES = 128
SUBLANES = 8
GDN_HEADS = 8
GDN_DK = 128
GDN_CONV = 4
SWA_HEAD_DIM = 64
SWA_Q_HEADS = 16
SWA_KV_HEADS = 4
SWA_WINDOW = 128
SWA_BLOCKS_PER_STEP = 4
V7X_SCOPED_VMEM_BYTES = 60000 * 1024

GDN_CHUNK = 128
GDN_SLAB_GROUP = 2
CONV_HALO = SUBLANES


def _cparams(semantics, vmem_bytes):
    return pltpu.CompilerParams(dimension_semantics=semantics,
                                vmem_limit_bytes=min(int(vmem_bytes), V7X_SCOPED_VMEM_BYTES))


def _sigmoid(x):
    return 1.0 / (1.0 + jnp.exp2(x * (-LOG2_E)))


def _silu(x):
    return x * _sigmoid(x)


def _rms(x, eps=NORM_EPS):
    return x * lax.rsqrt(jnp.mean(x * x, axis=-1, keepdims=True) + eps)


def _bdot(a, b):
    return jnp.dot(a.astype(BF16), b.astype(BF16), preferred_element_type=F32)


def _bmm(a, b):
    return jnp.einsum("bij,bjk->bik", a.astype(BF16), b.astype(BF16), preferred_element_type=F32)


def _resident(shape):
    nd = len(shape)
    return pl.BlockSpec(shape, lambda *_: (0,) * nd, pipeline_mode=pl.Buffered(1))


def _layer_resident(arr, layer, block=None):
    block = tuple(arr.shape[1:]) if block is None else tuple(block)
    nd = len(block)
    return pl.BlockSpec((None,) + block, lambda *_: (layer,) + (0,) * nd, pipeline_mode=pl.Buffered(1))


def _mod_kernel(c_ref, w_ref, b_ref, o_ref):
    o_ref[0] = _bdot(_silu(c_ref[...]), w_ref[0]) + b_ref[0]


def modulation(c, w, b):
    L, D, N = w.shape
    B = c.shape[0]
    tn = min(N, 1024)
    assert N % tn == 0
    return pl.pallas_call(
        _mod_kernel,
        grid=(L, N // tn),
        in_specs=[pl.BlockSpec((B, D), lambda l, n: (0, 0)),
                  pl.BlockSpec((1, D, tn), lambda l, n: (l, 0, n)),
                  pl.BlockSpec((1, 1, tn), lambda l, n: (l, 0, n))],
        out_specs=pl.BlockSpec((1, B, tn), lambda l, n: (l, 0, n)),
        out_shape=jax.ShapeDtypeStruct((L, B, N), F32),
        compiler_params=_cparams(("arbitrary", "arbitrary"), 4 * D * tn * 4 + (4 << 20)),
        name="modulation",
    )(c, w, b.reshape(L, 1, N))


def _mod_spec(D, layer, k):
    return pl.BlockSpec((None, None, 1, D), lambda b, *_: (layer, b, 0, k))


def _adaln_pre(x, gain, shift, scale):
    return _rms(x) * gain * (1.0 + scale) + shift


def _gdn_gates(ab, alog_ref, dtb_ref, out_gb_ref, out_gbt_ref, chunk):
    tm = ab.shape[0]
    lane = lax.broadcasted_iota(jnp.int32, ab.shape, 1)
    pre = ab + dtb_ref[...]
    softplus = jnp.maximum(pre, 0.0) + jnp.log(1.0 + jnp.exp(-jnp.abs(pre)))
    g = -jnp.exp(alog_ref[...]) * softplus
    g = jnp.where(lane < GDN_HEADS, g, 0.0)
    beta = _sigmoid(ab)
    r = lax.broadcasted_iota(jnp.int32, (chunk, chunk), 0)
    c = lax.broadcasted_iota(jnp.int32, (chunk, chunk), 1)
    tri = (r >= c).astype(F32)
    parts = []
    for i in range(tm // chunk):
        parts.append(jnp.dot(tri, g[i * chunk:(i + 1) * chunk], preferred_element_type=F32,
                             precision=lax.Precision.HIGHEST))
    gcum = jnp.concatenate(parts, axis=0) if len(parts) > 1 else parts[0]
    gb = jnp.where(lane < GDN_HEADS, gcum, jnp.where(lane < 2 * GDN_HEADS, beta, 0.0))
    out_gb_ref[...] = gb
    out_gbt_ref[...] = gb.T[:2 * GDN_HEADS]


def _conv_epilogue(g0, slabs, convw_ref, stage_ref, buf, carry_ref, out_ref):
    group = len(slabs)
    tm = slabs[0].shape[0]
    stage_ref[buf, :, 0:CONV_HALO, :] = carry_ref[g0:g0 + group]
    for j in range(group):
        stage_ref[buf, j, CONV_HALO:CONV_HALO + tm, :] = slabs[j]
    acc = None
    for t in range(GDN_CONV):
        off = CONV_HALO - (GDN_CONV - 1) + t
        term = convw_ref[t, g0:g0 + group] * stage_ref[buf, :, off:off + tm, :]
        acc = term if acc is None else acc + term
    carry_ref[g0:g0 + group] = stage_ref[buf, :, tm:tm + CONV_HALO, :]
    y = _silu(acc)
    if g0 < 2 * GDN_HEADS:
        y = y * lax.rsqrt(jnp.sum(y * y, axis=-1, keepdims=True) + NORM_EPS)
    out_ref[g0:g0 + group] = y.astype(BF16)


def _gdn_group_order(n_slabs):
    group = GDN_SLAB_GROUP
    n_conv = 3 * GDN_HEADS // group
    conv_groups, z_groups = list(range(n_conv)), list(range(n_conv, n_slabs // group))
    per_z = -(-n_conv // max(len(z_groups), 1))
    order = []
    while conv_groups or z_groups:
        order += conv_groups[:per_z]
        conv_groups = conv_groups[per_z:]
        if z_groups:
            order.append(z_groups.pop(0))
    return order


def _gdn_in_body(x, first_row_block, gain, shift_ref, scale_ref, w_ref, wab_ref, alog_ref, dtb_ref,
                 convw_ref, p_ref, gb_ref, gbt_ref, stage_ref, carry_ref, chunk):
    group = GDN_SLAB_GROUP

    @pl.when(first_row_block)
    def _():
        carry_ref[...] = jnp.zeros_like(carry_ref)

    h = _adaln_pre(x, gain, shift_ref[...], scale_ref[...]).astype(BF16)

    def epilogue(gi, res):
        g0 = gi * group
        slabs = [res[:, j * LANES:(j + 1) * LANES] for j in range(group)]
        if g0 >= 3 * GDN_HEADS:
            for j in range(group):
                p_ref[g0 + j] = slabs[j].astype(BF16)
        else:
            _conv_epilogue(g0, slabs, convw_ref, stage_ref, gi % 2, carry_ref, p_ref)

    pending = None
    for gi in _gdn_group_order(w_ref.shape[1] // LANES):
        g0 = gi * group
        res = jnp.dot(h, w_ref[:, g0 * LANES:(g0 + group) * LANES], preferred_element_type=F32)
        if pending is not None:
            epilogue(*pending)
        pending = (gi, res)
    ab = jnp.dot(h, wab_ref[...], preferred_element_type=F32)
    epilogue(*pending)
    _gdn_gates(ab, alog_ref, dtb_ref, gb_ref, gbt_ref, chunk)


def _gdn_in_kernel(x_ref, gains_ref, shift_ref, scale_ref, w_ref, wab_ref, alog_ref, dtb_ref, convw_ref,
                   p_ref, gb_ref, gbt_ref, stage_ref, carry_ref, *, chunk):
    _gdn_in_body(x_ref[...], pl.program_id(1) == 0, gains_ref[0:1, :], shift_ref, scale_ref, w_ref, wab_ref,
                 alog_ref, dtb_ref, convw_ref, p_ref, gb_ref, gbt_ref, stage_ref, carry_ref, chunk)


def _gdn_in_outputs(B, S, tm, n_slabs):
    out_specs = [pl.BlockSpec((None, n_slabs, tm, LANES), lambda b, s: (b, 0, s, 0)),
                 pl.BlockSpec((None, tm, LANES), lambda b, s: (b, s, 0)),
                 pl.BlockSpec((None, 2 * GDN_HEADS, tm), lambda b, s: (b, 0, s))]
    out_shape = [jax.ShapeDtypeStruct((B, n_slabs, S, LANES), BF16),
                 jax.ShapeDtypeStruct((B, S, LANES), F32),
                 jax.ShapeDtypeStruct((B, 2 * GDN_HEADS, S), F32)]
    return out_specs, out_shape


def _gdn_in_scratch(tm):
    return [pltpu.VMEM((2, GDN_SLAB_GROUP, tm + CONV_HALO, LANES), F32),
            pltpu.VMEM((3 * GDN_HEADS, CONV_HALO, LANES), F32)]


def _gdn_in_scratch_bytes(tm):
    return 2 * GDN_SLAB_GROUP * (tm + CONV_HALO) * LANES * 4 + 3 * GDN_HEADS * CONV_HALO * LANES * 4


def _gdn_params(w_in, conv_w, a_log, dt_bias):
    n, D, n_in = w_in.shape
    main = n_in - 2 * GDN_HEADS
    w_all = w_in.astype(BF16)
    w_ab = jnp.zeros((n, D, LANES), F32).at[:, :, :2 * GDN_HEADS].set(w_in[:, :, main:]).astype(BF16)
    alog = jnp.zeros((n, 1, LANES), F32).at[:, 0, :GDN_HEADS].set(a_log.astype(F32))
    dtb = jnp.zeros((n, 1, LANES), F32).at[:, 0, :GDN_HEADS].set(dt_bias.astype(F32))
    convw = conv_w.astype(F32).reshape(n, GDN_CONV, main // LANES - GDN_HEADS, 1, LANES)
    return dict(w_all=w_all, main=main, w_ab=w_ab, alog=alog, dtb=dtb, convw=convw)


def _gdn_in_inputs(gp, i):
    D = gp["w_all"].shape[1]
    ins = [gp["w_all"], gp["w_ab"], gp["alog"], gp["dtb"], gp["convw"]]
    specs = [_layer_resident(gp["w_all"], i, (D, gp["main"])), _layer_resident(gp["w_ab"], i),
             _layer_resident(gp["alog"], i), _layer_resident(gp["dtb"], i), _layer_resident(gp["convw"], i)]
    weight_bytes = (D * gp["main"] + D * LANES) * 2 + gp["convw"][0].size * 4 * SUBLANES
    return ins, specs, weight_bytes


def gdn_in_proj(x, mods, layer, gains, gp, i, tm):
    B, S, D = x.shape
    n_slabs = gp["main"] // LANES
    out_specs, out_shape = _gdn_in_outputs(B, S, tm, n_slabs)
    w_ins, w_specs, weight_bytes = _gdn_in_inputs(gp, i)
    vmem = (2 * tm * D * 4 + weight_bytes + 2 * tm * n_slabs * LANES * 2 + _gdn_in_scratch_bytes(tm)
            + 6 * tm * GDN_SLAB_GROUP * LANES * 4 + (6 << 20))
    return pl.pallas_call(
        functools.partial(_gdn_in_kernel, chunk=GDN_CHUNK),
        grid=(B, S // tm),
        in_specs=[pl.BlockSpec((None, tm, D), lambda b, s: (b, s, 0)),
                  _layer_resident(gains, layer),
                  _mod_spec(D, layer, 0), _mod_spec(D, layer, 1)] + w_specs,
        out_specs=out_specs, out_shape=out_shape,
        scratch_shapes=_gdn_in_scratch(tm),
        compiler_params=_cparams(("arbitrary", "arbitrary"), vmem),
        name="gdn_in_proj",
    )(x, gains, mods, mods, *w_ins)


def _unit_lower_inverse(lmat, ri, ci):
    n = lmat.shape[-1]
    diff = ri ^ ci
    p = jnp.where(ri == ci, 1.0, 0.0) - jnp.where(diff < 2, lmat, 0.0)
    s = 2
    while s < n:
        off = jnp.where(diff >= s, jnp.where(diff < 2 * s, lmat, 0.0), 0.0)
        p = p - _bmm(p, _bmm(off, p))
        s *= 2
    return p


def _gdn_core_kernel(q_ref, k_ref, v_ref, gb_ref, gbt_ref, o_ref, state_ref, *, rows, chunk):
    H = GDN_HEADS
    nc = rows // chunk

    @pl.when(pl.program_id(1) == 0)
    def _():
        state_ref[...] = jnp.zeros_like(state_ref)

    def chunked(t):
        if nc == 1:
            return t
        return jnp.concatenate([t[:, c * chunk:(c + 1) * chunk] for c in range(nc)], axis=0)

    q = chunked(q_ref[...])
    k = chunked(k_ref[...])
    v = chunked(v_ref[...])

    gb = gb_ref[...]
    gbt = gbt_ref[...]
    gcc = chunked(jnp.stack([gb[:, h:h + 1] for h in range(H)]))
    bc = chunked(jnp.stack([gb[:, H + h:H + h + 1] for h in range(H)]))

    def rows_of(base):
        return jnp.stack([gbt[base + h:base + h + 1, c * chunk:(c + 1) * chunk]
                          for c in range(nc) for h in range(H)])

    grc, brc = rows_of(0), rows_of(H)
    g_last = gcc[:, chunk - 1:chunk, :]

    ri = lax.broadcasted_iota(jnp.int32, (1, chunk, chunk), 1)
    ci = lax.broadcasted_iota(jnp.int32, (1, chunk, chunk), 2)
    gcb = jnp.broadcast_to(gcc, (nc * H, chunk, LANES))
    decay = jnp.exp(jnp.where(ri >= ci, gcb[:, :, :chunk] - grc, -jnp.inf))
    kk_qk = jnp.einsum("bcd,bsd->bcs", jnp.concatenate([k, q], axis=1), k,
                       preferred_element_type=F32)
    lmat = jnp.where(ri > ci, kk_qk[:, :chunk] * decay * bc, 0.0)
    attn = (kk_qk[:, chunk:] * decay).astype(BF16)
    tinv = _unit_lower_inverse(lmat, ri, ci)
    u = _bmm(tinv * brc, v)
    w = _bmm(tinv * (brc * jnp.exp(grc)), k)
    wq = jnp.concatenate([w.astype(BF16), q], axis=1)
    eg = jnp.exp(gcb)
    kt = jnp.stack([k[b].astype(F32).T for b in range(nc * H)])
    kgt = (kt * jnp.exp(g_last - grc)).astype(BF16)
    egl = jnp.exp(g_last)

    state = state_ref[...]
    for c in range(nc):
        sel = slice(c * H, (c + 1) * H)
        ws_qs = _bmm(wq[sel], state)
        v_new = (u[sel] - ws_qs[:, :chunk]).astype(BF16)
        o = ws_qs[:, chunk:] * eg[sel] + _bmm(attn[sel], v_new)
        state = state * egl[sel] + _bmm(kgt[sel], v_new)
        o_ref[:, c * chunk:(c + 1) * chunk, :] = o.astype(BF16)
    state_ref[...] = state


def gdn_core(p, gb, gbt, rows):
    B, _, S, _ = p.shape
    H = GDN_HEADS

    def slabs(group):
        return pl.BlockSpec((None, H, rows, LANES), lambda b, s: (b, group, s, 0))

    return pl.pallas_call(
        functools.partial(_gdn_core_kernel, rows=rows, chunk=GDN_CHUNK),
        grid=(B, S // rows),
        in_specs=[slabs(0), slabs(1), slabs(2),
                  pl.BlockSpec((None, rows, LANES), lambda b, s: (b, s, 0)),
                  pl.BlockSpec((None, 2 * H, rows), lambda b, s: (b, 0, s))],
        out_specs=pl.BlockSpec((None, H, rows, LANES), lambda b, s: (b, 0, s, 0)),
        out_shape=jax.ShapeDtypeStruct((B, H, S, LANES), BF16),
        scratch_shapes=[pltpu.VMEM((H, GDN_DK, LANES), F32)],
        compiler_params=_cparams(("arbitrary", "arbitrary"), 32 << 20),
        name="gdn_core",
    )(p, p, p, gb, gbt)


def _alibi_slope(head):
    return 2.0 ** (-8.0 * (head + 1) / SWA_Q_HEADS)


def _swa_kernel(sink_ref, qt_ref, kp_ref, kc_ref, vtp_ref, vtc_ref, o_ref, bias_ref, *, layer_slot, n_blocks):
    blk = pl.program_id(1)
    W = SWA_WINDOW
    hd = SWA_HEAD_DIM
    KV = SWA_KV_HEADS
    G = SWA_Q_HEADS // KV
    NEG = -jnp.inf

    @pl.when(jnp.logical_and(pl.program_id(0) == 0, blk == 0))
    def _():
        key = lax.broadcasted_iota(jnp.int32, (2 * W, G * W), 0)
        col = lax.broadcasted_iota(jnp.int32, (2 * W, G * W), 1)
        dist = (col & (W - 1)) + W - key
        valid = (dist >= 0) & (dist < SWA_WINDOW)
        distf = dist.astype(F32)
        for j in range(KV):
            slope = jnp.zeros((2 * W, G * W), F32)
            for a in range(G):
                slope = jnp.where((col >= a * W) & (col < (a + 1) * W), _alibi_slope(G * j + a), slope)
            bias = jnp.where(valid, -slope * distf, NEG)
            bias_ref[1, j] = bias
            bias_ref[0, j] = jnp.where(key >= W, bias, NEG)

    kall = jnp.concatenate([kp_ref[...], kc_ref[...]], axis=0)
    vtall = jnp.concatenate([vtp_ref[...], vtc_ref[...]], axis=1)
    zpad = jnp.zeros((hd, W), BF16)
    ks, ws, vts = [], [], []
    for i in range(n_blocks):
        for j in range(KV):
            tile, half = divmod(j, LANES // hd)
            ks.append(kall[i * W:(i + 2) * W, tile * LANES:(tile + 1) * LANES])
            cols = []
            for a in range(G):
                h = G * j + a
                qt = qt_ref[h * hd:(h + 1) * hd, i * W:(i + 1) * W]
                cols.append(jnp.concatenate([qt, zpad] if half == 0 else [zpad, qt], axis=0))
            ws.append(jnp.concatenate(cols, axis=1))
            vts.append(vtall[j * hd:(j + 1) * hd, i * W:(i + 2) * W])
    ks, ws, vts = jnp.stack(ks), jnp.stack(ws), jnp.stack(vts)

    s = jnp.einsum("jkd,jdq->jkq", ks, ws, preferred_element_type=F32)
    bias = [bias_ref[jnp.minimum(blk, 1)]] + [bias_ref[1]] * (n_blocks - 1)
    s = s + (jnp.concatenate(bias, axis=0) if n_blocks > 1 else bias[0])
    lane = lax.broadcasted_iota(jnp.int32, (1, G * W), 1)
    sinks = []
    for j in range(KV):
        row = jnp.zeros((1, G * W), F32)
        for a in range(G):
            row = jnp.where((lane >= a * W) & (lane < (a + 1) * W), sink_ref[layer_slot, G * j + a], row)
        sinks.append(row)
    sink = jnp.stack(sinks * n_blocks)
    m = jnp.maximum(jnp.max(s, axis=1, keepdims=True), sink)
    p = jnp.exp(s - m)
    denom = jnp.sum(p, axis=1, keepdims=True) + jnp.exp(sink - m)
    ot = jnp.einsum("jdk,jkq->jdq", vts, p.astype(BF16), preferred_element_type=F32)
    ot = ot * (1.0 / denom)
    for i in range(n_blocks):
        for t in range(SWA_Q_HEADS // 2):
            j, a0 = divmod(2 * t, G)
            oj = ot[i * KV + j]
            pair = jnp.concatenate([oj[:, a0 * W:(a0 + 1) * W], oj[:, (a0 + 1) * W:(a0 + 2) * W]], axis=0)
            o_ref[i * W:(i + 1) * W, t * LANES:(t + 1) * LANES] = pair.T.astype(BF16)


def swa_attention(qt, k_sh, vt_sh, sinks, layer_slot):
    B, QW, S = qt.shape
    KW = k_sh.shape[-1]
    W = SWA_WINDOW
    G = SWA_Q_HEADS // SWA_KV_HEADS
    nb = SWA_BLOCKS_PER_STEP if S % (SWA_BLOCKS_PER_STEP * W) == 0 else 1
    T = nb * W
    prev = lambda n: jnp.maximum(n * nb - 1, 0)
    return pl.pallas_call(
        functools.partial(_swa_kernel, layer_slot=layer_slot, n_blocks=nb),
        grid=(B, S // T),
        in_specs=[pl.BlockSpec(memory_space=pltpu.SMEM),
                  pl.BlockSpec((None, QW, T), lambda b, n: (b, 0, n)),
                  pl.BlockSpec((None, W, KW), lambda b, n: (b, prev(n), 0)),
                  pl.BlockSpec((None, T, KW), lambda b, n: (b, n, 0)),
                  pl.BlockSpec((None, KW, W), lambda b, n: (b, 0, prev(n))),
                  pl.BlockSpec((None, KW, T), lambda b, n: (b, 0, n))],
        out_specs=pl.BlockSpec((None, T, QW), lambda b, n: (b, n, 0)),
        out_shape=jax.ShapeDtypeStruct((B, S, QW), BF16),
        scratch_shapes=[pltpu.VMEM((2, SWA_KV_HEADS, 2 * W, G * W), F32)],
        compiler_params=_cparams(("arbitrary", "arbitrary"), 32 << 20),
        name="swa_attention",
    )(sinks, qt, k_sh, k_sh, vt_sh, vt_sh)


def _post_mlp_kernel(*refs, gated, tail, ff_chunk, gdn_chunk):
    n_fixed = 12 if gated else 10
    if gated:
        (x_ref, a_ref, z_ref, onorm_ref, wo_ref, w1_ref, w2_ref, gains_ref,
         gate_mix_ref, shift_mlp_ref, scale_mlp_ref, gate_mlp_ref) = refs[:n_fixed]
    else:
        (x_ref, a_ref, wo_ref, w1_ref, w2_ref, gains_ref,
         gate_mix_ref, shift_mlp_ref, scale_mlp_ref, gate_mlp_ref) = refs[:n_fixed]
    rest = refs[n_fixed:]
    x = x_ref[...]
    if gated:
        g = (_rms(a_ref[...].astype(F32), NORM_EPS * GDN_DK) * onorm_ref[...]
             * _silu(z_ref[...].astype(F32))).astype(BF16)
        act = jnp.concatenate([g[h] for h in range(g.shape[0])], axis=1)
    else:
        act = a_ref[...]
    y = jnp.dot(act, wo_ref[...], preferred_element_type=F32)
    x = x + (1.0 + gate_mix_ref[...]) * (_rms(y) * gains_ref[1:2, :])
    h = _adaln_pre(x, gains_ref[2:3, :], shift_mlp_ref[...], scale_mlp_ref[...]).astype(BF16)
    d_ff = w1_ref.shape[1]
    acc = None
    for f0 in range(0, d_ff, ff_chunk):
        a = jnp.dot(h, w1_ref[:, f0:f0 + ff_chunk], preferred_element_type=F32)
        a = jnp.square(jnp.maximum(a, 0.0)).astype(BF16)
        part = jnp.dot(a, w2_ref[f0:f0 + ff_chunk, :], preferred_element_type=F32)
        acc = part if acc is None else acc + part
    x = x + (1.0 + gate_mlp_ref[...]) * (_rms(acc) * gains_ref[3:4, :])

    def q_proj(ngains_ref, nshift_ref, nscale_ref, wq_ref, q_ref):
        hq = _adaln_pre(x, ngains_ref[0:1, :], nshift_ref[...], nscale_ref[...]).astype(BF16)
        q = jnp.dot(hq, wq_ref[...], preferred_element_type=F32) * (SWA_HEAD_DIM ** -0.5)
        q_ref[...] = q.T.astype(BF16)

    if tail == "none":
        (xo_ref,) = rest
        xo_ref[...] = x
    elif tail == "gdn_in":
        (ngains_ref, nshift_ref, nscale_ref, w_ref, wab_ref, alog_ref, dtb_ref, convw_ref,
         xo_ref, p_ref, gb_ref, gbt_ref, stage_ref, carry_ref) = rest
        xo_ref[...] = x
        _gdn_in_body(x, pl.program_id(1) == 0, ngains_ref[0:1, :], nshift_ref, nscale_ref, w_ref, wab_ref,
                     alog_ref, dtb_ref, convw_ref, p_ref, gb_ref, gbt_ref, stage_ref, carry_ref, gdn_chunk)
    elif tail == "kv_q":
        (kvgain_ref, kvshift_ref, kvscale_ref, wkv_ref, ngains_ref, nshift_ref, nscale_ref, wq_ref,
         xo_ref, k_ref, v_ref, q_ref) = rest
        xo_ref[...] = x
        hk = _adaln_pre(x, kvgain_ref[...], kvshift_ref[...], kvscale_ref[...]).astype(BF16)
        kv = jnp.dot(hk, wkv_ref[...], preferred_element_type=F32)
        kvw = kv.shape[1] // 2
        k_ref[...] = kv[:, :kvw].astype(BF16)
        v_ref[...] = kv[:, kvw:].T.astype(BF16)
        q_proj(ngains_ref, nshift_ref, nscale_ref, wq_ref, q_ref)
    elif tail == "q":
        (ngains_ref, nshift_ref, nscale_ref, wq_ref, xo_ref, q_ref) = rest
        xo_ref[...] = x
        q_proj(ngains_ref, nshift_ref, nscale_ref, wq_ref, q_ref)
    else:
        raise ValueError(tail)


def post_mlp(x, act, mods, layer, gains, w_o, wo_idx, w1, w2, tm, *, gate_args=None,
             tail="none", tail_args=None):
    B, S, D = x.shape
    gated = gate_args is not None
    row = lambda width: pl.BlockSpec((None, tm, width), lambda b, s: (b, s, 0))
    col = lambda width: pl.BlockSpec((None, width, tm), lambda b, s: (b, 0, s))
    ins, in_specs = [x], [row(D)]
    if gated:
        z_src, z_group, onorm, oidx = gate_args
        H = act.shape[1]
        ins += [act, z_src, onorm]
        in_specs += [pl.BlockSpec((None, H, tm, LANES), lambda b, s: (b, 0, s, 0)),
                     pl.BlockSpec((None, H, tm, LANES), lambda b, s: (b, z_group, s, 0)),
                     _layer_resident(onorm, oidx)]
    else:
        ins += [act]
        in_specs += [row(act.shape[-1])]
    ins += [w_o, w1, w2, gains, mods, mods, mods, mods]
    in_specs += [_layer_resident(w_o, wo_idx), _layer_resident(w1, layer), _layer_resident(w2, layer),
                 _layer_resident(gains, layer),
                 _mod_spec(D, layer, 2), _mod_spec(D, layer, 3), _mod_spec(D, layer, 4), _mod_spec(D, layer, 5)]
    out_specs = [row(D)]
    out_shape = [jax.ShapeDtypeStruct((B, S, D), F32)]
    scratch = []
    weight_bytes = (w_o[0].size + w1[0].size + w2[0].size) * 2
    extra = 0
    if tail == "gdn_in":
        nlayer, gp, gi = tail_args
        w_ins, w_specs, wb = _gdn_in_inputs(gp, gi)
        ins += [gains, mods, mods] + w_ins
        in_specs += [_layer_resident(gains, nlayer), _mod_spec(D, nlayer, 0), _mod_spec(D, nlayer, 1)] + w_specs
        n_slabs = gp["main"] // LANES
        os_, osh = _gdn_in_outputs(B, S, tm, n_slabs)
        out_specs += os_
        out_shape += osh
        scratch = _gdn_in_scratch(tm)
        weight_bytes += wb
        extra = 2 * tm * n_slabs * LANES * 2 + _gdn_in_scratch_bytes(tm)
    elif tail == "kv_q":
        kvmods, kvgain, w_kv, nlayer, w_q, qi = tail_args
        kv_spec = lambda k: pl.BlockSpec((None, None, 1, D), lambda b, s: (0, b, 0, k))
        ins += [kvgain.reshape(1, D), kvmods, kvmods, w_kv, gains, mods, mods, w_q]
        in_specs += [_resident((1, D)), kv_spec(0), kv_spec(1), _resident(w_kv.shape),
                     _layer_resident(gains, nlayer), _mod_spec(D, nlayer, 0), _mod_spec(D, nlayer, 1),
                     _layer_resident(w_q, qi)]
        kvw = w_kv.shape[1] // 2
        qw = w_q.shape[-1]
        out_specs += [row(kvw), col(kvw), col(qw)]
        out_shape += [jax.ShapeDtypeStruct((B, S, kvw), BF16), jax.ShapeDtypeStruct((B, kvw, S), BF16),
                      jax.ShapeDtypeStruct((B, qw, S), BF16)]
        weight_bytes += (w_kv.size + w_q[0].size) * 2
        extra = 2 * tm * (2 * kvw + qw) * 2
    elif tail == "q":
        nlayer, w_q, qi = tail_args
        qw = w_q.shape[-1]
        ins += [gains, mods, mods, w_q]
        in_specs += [_layer_resident(gains, nlayer), _mod_spec(D, nlayer, 0), _mod_spec(D, nlayer, 1),
                     _layer_resident(w_q, qi)]
        out_specs += [col(qw)]
        out_shape += [jax.ShapeDtypeStruct((B, qw, S), BF16)]
        weight_bytes += w_q[0].size * 2
        extra = 2 * tm * qw * 2
    ff_chunk = 512
    vmem = (weight_bytes + extra + 4 * tm * D * 4 + 4 * tm * D * 2 + 6 * tm * D * 4
            + 3 * tm * ff_chunk * 4 + (4 << 20))
    return pl.pallas_call(
        functools.partial(_post_mlp_kernel, gated=gated, tail=tail, ff_chunk=ff_chunk, gdn_chunk=GDN_CHUNK),
        grid=(B, S // tm),
        in_specs=in_specs, out_specs=out_specs, out_shape=out_shape,
        scratch_shapes=scratch,
        compiler_params=_cparams(("arbitrary", "arbitrary"), vmem),
        name="post_mlp_" + tail,
    )(*ins)


def kernel(x, c, mod_w, mod_b, norm_g, gdn_w_in, gdn_conv, gdn_a_log, gdn_dt_bias, gdn_onorm,
           gdn_w_out, kv_mod_w, kv_mod_b, kv_norm, w_kv, swa_w_q, swa_sinks, swa_w_o,
           mlp_w1, mlp_w2):
    B, S, D = x.shape
    depth = mod_w.shape[0]
    n_gdn = gdn_w_in.shape[0]
    tm = min(S, 512)
    tm_wide = min(S, 1024)
    rows = min(S, 256)
    assert S % tm == 0 and S % rows == 0 and rows % GDN_CHUNK == 0 and S % SWA_WINDOW == 0

    mods = modulation(c, mod_w, mod_b).reshape(depth, B, 1, mod_w.shape[-1])
    kvmods = modulation(c, kv_mod_w[None], kv_mod_b[None]).reshape(1, B, 1, kv_mod_w.shape[-1])

    w1 = mlp_w1.astype(BF16)
    w2 = mlp_w2.astype(BF16)
    gp = _gdn_params(gdn_w_in, gdn_conv, gdn_a_log, gdn_dt_bias)
    gdn_wo = gdn_w_out.astype(BF16)
    swa_wq = swa_w_q.astype(BF16)
    swa_wo = swa_w_o.astype(BF16)
    wkv = w_kv.astype(BF16)
    sinks = swa_sinks.astype(F32)
    gains = norm_g.astype(F32)
    onorm = gdn_onorm.astype(F32).reshape(n_gdn, 1, LANES)

    p, gb, gbt = gdn_in_proj(x, mods, 0, gains, gp, 0, tm)
    z_group = 3
    k_sh = v_sh = q = None
    for layer in range(depth):
        nxt = layer + 1
        if layer < n_gdn:
            act = gdn_core(p, gb, gbt, rows)
            w_o, wo_idx, gate_args = gdn_wo, layer, (p, z_group, onorm, layer)
        else:
            act = swa_attention(q, k_sh, v_sh, sinks, layer - n_gdn)
            w_o, wo_idx, gate_args = swa_wo, layer - n_gdn, None
        if nxt == depth:
            tail, tail_args = "none", None
        elif nxt < n_gdn:
            tail, tail_args = "gdn_in", (nxt, gp, nxt)
        elif nxt == n_gdn:
            tail, tail_args = "kv_q", (kvmods, kv_norm, wkv, nxt, swa_wq, 0)
        else:
            tail, tail_args = "q", (nxt, swa_wq, nxt - n_gdn)
        outs = post_mlp(x, act, mods, layer, gains, w_o, wo_idx, w1, w2, tm if gate_args is not None else tm_wide,
                        gate_args=gate_args, tail=tail, tail_args=tail_args)
        x = outs[0]
        if tail == "gdn_in":
            p, gb, gbt = outs[1:]
        elif tail == "kv_q":
            k_sh, v_sh, q = outs[1:]
        elif tail == "q":
            (q,) = outs[1:]
    return x
```

```python
import functools

import jax
import jax.numpy as jnp
from jax import lax
from jax.experimental import pallas as pl
from jax.experimental.pallas import tpu as pltpu

F32 = jnp.float32
BF16 = jnp.bfloat16

NORM_EPS = 1e-6
LOG2_E = 1.4426950408889634
LANES = 128
SUBLANES = 8
GDN_HEADS = 8
GDN_DK = 128
GDN_CONV = 4
SWA_HEAD_DIM = 64
SWA_Q_HEADS = 16
SWA_KV_HEADS = 4
SWA_WINDOW = 128
SWA_BLOCKS_PER_STEP = 8
V7X_SCOPED_VMEM_BYTES = 60000 * 1024

GDN_CHUNK = 128
GDN_SLAB_GROUP = 2
CONV_HALO = SUBLANES


def _cparams(semantics, vmem_bytes):
    return pltpu.CompilerParams(dimension_semantics=semantics,
                                vmem_limit_bytes=min(int(vmem_bytes), V7X_SCOPED_VMEM_BYTES))


def _sigmoid(x):
    return 1.0 / (1.0 + jnp.exp2(x * (-LOG2_E)))


def _silu(x):
    return x * _sigmoid(x)


def _rms(x, eps=NORM_EPS):
    return x * lax.rsqrt(jnp.mean(x * x, axis=-1, keepdims=True) + eps)


def _bdot(a, b):
    return jnp.dot(a.astype(BF16), b.astype(BF16), preferred_element_type=F32)


def _bmm(a, b):
    return jnp.einsum("bij,bjk->bik", a.astype(BF16), b.astype(BF16), preferred_element_type=F32)


def _resident(shape):
    nd = len(shape)
    return pl.BlockSpec(shape, lambda *_: (0,) * nd, pipeline_mode=pl.Buffered(1))


def _layer_resident(arr, layer, block=None):
    block = tuple(arr.shape[1:]) if block is None else tuple(block)
    nd = len(block)
    return pl.BlockSpec((None,) + block, lambda *_: (layer,) + (0,) * nd, pipeline_mode=pl.Buffered(1))


def _mod_kernel(c_ref, w_ref, b_ref, o_ref):
    o_ref[0] = _bdot(_silu(c_ref[...]), w_ref[0]) + b_ref[0]


def modulation(c, w, b):
    L, D, N = w.shape
    B = c.shape[0]
    tn = min(N, 1024)
    assert N % tn == 0
    return pl.pallas_call(
        _mod_kernel,
        grid=(L, N // tn),
        in_specs=[pl.BlockSpec((B, D), lambda l, n: (0, 0)),
                  pl.BlockSpec((1, D, tn), lambda l, n: (l, 0, n)),
                  pl.BlockSpec((1, 1, tn), lambda l, n: (l, 0, n))],
        out_specs=pl.BlockSpec((1, B, tn), lambda l, n: (l, 0, n)),
        out_shape=jax.ShapeDtypeStruct((L, B, N), F32),
        compiler_params=_cparams(("arbitrary", "arbitrary"), 4 * D * tn * 4 + (4 << 20)),
        name="modulation",
    )(c, w, b.reshape(L, 1, N))


def _mod_spec(D, layer, k):
    return pl.BlockSpec((None, None, 1, D), lambda b, *_: (layer, b, 0, k))


def _adaln_pre(x, gain, shift, scale):
    return _rms(x) * (gain * (1.0 + scale)) + shift


def _adaln_post(x, y, gain, gate):
    return x + _rms(y) * ((1.0 + gate) * gain)


def _gdn_gates(ab, alog_ref, dtb_ref, out_gb_ref, out_gbt_ref, chunk):
    tm = ab.shape[0]
    lane = lax.broadcasted_iota(jnp.int32, ab.shape, 1)
    pre = ab + dtb_ref[...]
    softplus = jnp.maximum(pre, 0.0) + jnp.log(1.0 + jnp.exp(-jnp.abs(pre)))
    g = -jnp.exp(alog_ref[...]) * softplus
    g = jnp.where(lane < GDN_HEADS, g, 0.0)
    beta = _sigmoid(ab)
    r = lax.broadcasted_iota(jnp.int32, (chunk, chunk), 0)
    c = lax.broadcasted_iota(jnp.int32, (chunk, chunk), 1)
    tri = (r >= c).astype(F32)
    parts = []
    for i in range(tm // chunk):
        parts.append(jnp.dot(tri, g[i * chunk:(i + 1) * chunk], preferred_element_type=F32,
                             precision=lax.Precision.HIGHEST))
    gcum = jnp.concatenate(parts, axis=0) if len(parts) > 1 else parts[0]
    gb = jnp.where(lane < GDN_HEADS, gcum, jnp.where(lane < 2 * GDN_HEADS, beta, 0.0))
    out_gb_ref[...] = gb
    out_gbt_ref[...] = gb.T[:2 * GDN_HEADS]


def _conv_epilogue(g0, slabs, convw_ref, stage_ref, buf, carry_ref, out_ref):
    group = len(slabs)
    tm = slabs[0].shape[0]
    stage_ref[buf, :, 0:CONV_HALO, :] = carry_ref[g0:g0 + group]
    for j in range(group):
        stage_ref[buf, j, CONV_HALO:CONV_HALO + tm, :] = slabs[j]
    acc = None
    for t in range(GDN_CONV):
        off = CONV_HALO - (GDN_CONV - 1) + t
        term = convw_ref[t, g0:g0 + group] * stage_ref[buf, :, off:off + tm, :]
        acc = term if acc is None else acc + term
    carry_ref[g0:g0 + group] = stage_ref[buf, :, tm:tm + CONV_HALO, :]
    y = _silu(acc)
    if g0 < 2 * GDN_HEADS:
        y = y * lax.rsqrt(jnp.sum(y * y, axis=-1, keepdims=True) + NORM_EPS)
    out_ref[g0:g0 + group] = y.astype(BF16)


def _gdn_group_order(n_slabs):
    group = GDN_SLAB_GROUP
    n_conv = 3 * GDN_HEADS // group
    conv_groups, z_groups = list(range(n_conv)), list(range(n_conv, n_slabs // group))
    per_z = -(-n_conv // max(len(z_groups), 1))
    order = []
    while conv_groups or z_groups:
        order += conv_groups[:per_z]
        conv_groups = conv_groups[per_z:]
        if z_groups:
            order.append(z_groups.pop(0))
    return order


def _gdn_in_body(x, first_row_block, gain, shift_ref, scale_ref, w_ref, wab_ref, alog_ref, dtb_ref,
                 convw_ref, p_ref, gb_ref, gbt_ref, stage_ref, carry_ref, chunk):
    group = GDN_SLAB_GROUP

    @pl.when(first_row_block)
    def _():
        carry_ref[...] = jnp.zeros_like(carry_ref)

    h = _adaln_pre(x, gain, shift_ref[...], scale_ref[...]).astype(BF16)

    def epilogue(gi, res):
        g0 = gi * group
        slabs = [res[:, j * LANES:(j + 1) * LANES] for j in range(group)]
        if g0 >= 3 * GDN_HEADS:
            for j in range(group):
                p_ref[g0 + j] = slabs[j].astype(BF16)
        else:
            _conv_epilogue(g0, slabs, convw_ref, stage_ref, gi % 2, carry_ref, p_ref)

    pending = None
    for gi in _gdn_group_order(w_ref.shape[1] // LANES):
        g0 = gi * group
        res = jnp.dot(h, w_ref[:, g0 * LANES:(g0 + group) * LANES], preferred_element_type=F32)
        if pending is not None:
            epilogue(*pending)
        pending = (gi, res)
    ab = jnp.dot(h, wab_ref[...], preferred_element_type=F32)
    epilogue(*pending)
    _gdn_gates(ab, alog_ref, dtb_ref, gb_ref, gbt_ref, chunk)


def _gdn_in_kernel(x_ref, gains_ref, shift_ref, scale_ref, w_ref, wab_ref, alog_ref, dtb_ref, convw_ref,
                   p_ref, gb_ref, gbt_ref, stage_ref, carry_ref, *, chunk):
    _gdn_in_body(x_ref[...], pl.program_id(1) == 0, gains_ref[0:1, :], shift_ref, scale_ref, w_ref, wab_ref,
                 alog_ref, dtb_ref, convw_ref, p_ref, gb_ref, gbt_ref, stage_ref, carry_ref, chunk)


def _gdn_in_outputs(B, S, tm, n_slabs):
    out_specs = [pl.BlockSpec((None, n_slabs, tm, LANES), lambda b, s: (b, 0, s, 0)),
                 pl.BlockSpec((None, tm, LANES), lambda b, s: (b, s, 0)),
                 pl.BlockSpec((None, 2 * GDN_HEADS, tm), lambda b, s: (b, 0, s))]
    out_shape = [jax.ShapeDtypeStruct((B, n_slabs, S, LANES), BF16),
                 jax.ShapeDtypeStruct((B, S, LANES), F32),
                 jax.ShapeDtypeStruct((B, 2 * GDN_HEADS, S), F32)]
    return out_specs, out_shape


def _gdn_in_scratch(tm):
    return [pltpu.VMEM((2, GDN_SLAB_GROUP, tm + CONV_HALO, LANES), F32),
            pltpu.VMEM((3 * GDN_HEADS, CONV_HALO, LANES), F32)]


def _gdn_in_scratch_bytes(tm):
    return 2 * GDN_SLAB_GROUP * (tm + CONV_HALO) * LANES * 4 + 3 * GDN_HEADS * CONV_HALO * LANES * 4


def _gdn_params(w_in, conv_w, a_log, dt_bias):
    n, D, n_in = w_in.shape
    main = n_in - 2 * GDN_HEADS
    w_all = w_in.astype(BF16)
    w_ab = jnp.zeros((n, D, LANES), F32).at[:, :, :2 * GDN_HEADS].set(w_in[:, :, main:]).astype(BF16)
    alog = jnp.zeros((n, 1, LANES), F32).at[:, 0, :GDN_HEADS].set(a_log.astype(F32))
    dtb = jnp.zeros((n, 1, LANES), F32).at[:, 0, :GDN_HEADS].set(dt_bias.astype(F32))
    convw = conv_w.astype(F32).reshape(n, GDN_CONV, main // LANES - GDN_HEADS, 1, LANES)
    return dict(w_all=w_all, main=main, w_ab=w_ab, alog=alog, dtb=dtb, convw=convw)


def _gdn_in_inputs(gp, i):
    D = gp["w_all"].shape[1]
    ins = [gp["w_all"], gp["w_ab"], gp["alog"], gp["dtb"], gp["convw"]]
    specs = [_layer_resident(gp["w_all"], i, (D, gp["main"])), _layer_resident(gp["w_ab"], i),
             _layer_resident(gp["alog"], i), _layer_resident(gp["dtb"], i), _layer_resident(gp["convw"], i)]
    weight_bytes = (D * gp["main"] + D * LANES) * 2 + gp["convw"][0].size * 4 * SUBLANES
    return ins, specs, weight_bytes


def gdn_in_proj(x, mods, layer, gains, gp, i, tm):
    B, S, D = x.shape
    n_slabs = gp["main"] // LANES
    out_specs, out_shape = _gdn_in_outputs(B, S, tm, n_slabs)
    w_ins, w_specs, weight_bytes = _gdn_in_inputs(gp, i)
    vmem = (2 * tm * D * 4 + weight_bytes + 2 * tm * n_slabs * LANES * 2 + _gdn_in_scratch_bytes(tm)
            + 6 * tm * GDN_SLAB_GROUP * LANES * 4 + (6 << 20))
    return pl.pallas_call(
        functools.partial(_gdn_in_kernel, chunk=GDN_CHUNK),
        grid=(B, S // tm),
        in_specs=[pl.BlockSpec((None, tm, D), lambda b, s: (b, s, 0)),
                  _layer_resident(gains, layer),
                  _mod_spec(D, layer, 0), _mod_spec(D, layer, 1)] + w_specs,
        out_specs=out_specs, out_shape=out_shape,
        scratch_shapes=_gdn_in_scratch(tm),
        compiler_params=_cparams(("arbitrary", "arbitrary"), vmem),
        name="gdn_in_proj",
    )(x, gains, mods, mods, *w_ins)


def _unit_lower_inverse(lmat, ri, ci):
    n = lmat.shape[-1]
    diff = ri ^ ci
    p = jnp.where(ri == ci, 1.0, 0.0) - jnp.where(diff < 2, lmat, 0.0)
    s = 2
    while s < n:
        off = jnp.where(diff >= s, jnp.where(diff < 2 * s, lmat, 0.0), 0.0)
        p = p - _bmm(p, _bmm(off, p))
        s *= 2
    return p


def _gdn_core_kernel(q_ref, k_ref, v_ref, gb_ref, gbt_ref, o_ref, state_ref, *, rows, chunk):
    H = GDN_HEADS
    nc = rows // chunk

    @pl.when(pl.program_id(1) == 0)
    def _():
        state_ref[...] = jnp.zeros_like(state_ref)

    def chunked(t):
        if nc == 1:
            return t
        return jnp.concatenate([t[:, c * chunk:(c + 1) * chunk] for c in range(nc)], axis=0)

    q = chunked(q_ref[...])
    k = chunked(k_ref[...])
    v = chunked(v_ref[...])

    gb = gb_ref[...]
    gbt = gbt_ref[...]
    gcc = chunked(jnp.stack([gb[:, h:h + 1] for h in range(H)]))
    bc = chunked(jnp.stack([gb[:, H + h:H + h + 1] for h in range(H)]))

    def rows_of(base):
        return jnp.stack([gbt[base + h:base + h + 1, c * chunk:(c + 1) * chunk]
                          for c in range(nc) for h in range(H)])

    grc, brc = rows_of(0), rows_of(H)
    g_last = gcc[:, chunk - 1:chunk, :]

    ri = lax.broadcasted_iota(jnp.int32, (1, chunk, chunk), 1)
    ci = lax.broadcasted_iota(jnp.int32, (1, chunk, chunk), 2)
    gcb = jnp.broadcast_to(gcc, (nc * H, chunk, LANES))
    decay = jnp.exp(jnp.where(ri >= ci, gcb[:, :, :chunk] - grc, -jnp.inf))
    kk_qk = jnp.einsum("bcd,bsd->bcs", jnp.concatenate([k, q], axis=1), k,
                       preferred_element_type=F32)
    lmat = jnp.where(ri > ci, kk_qk[:, :chunk] * decay * bc, 0.0)
    attn = (kk_qk[:, chunk:] * decay).astype(BF16)
    tinv = _unit_lower_inverse(lmat, ri, ci)
    u = _bmm(tinv * brc, v)
    w = _bmm(tinv * (brc * jnp.exp(grc)), k)
    wq = jnp.concatenate([w.astype(BF16), q], axis=1)
    eg = jnp.exp(gcb)
    kt = jnp.stack([k[b].astype(F32).T for b in range(nc * H)])
    kgt = (kt * jnp.exp(g_last - grc)).astype(BF16)
    egl = jnp.exp(g_last)

    state = state_ref[...]
    for c in range(nc):
        sel = slice(c * H, (c + 1) * H)
        ws_qs = _bmm(wq[sel], state)
        v_new = (u[sel] - ws_qs[:, :chunk]).astype(BF16)
        o = ws_qs[:, chunk:] * eg[sel] + _bmm(attn[sel], v_new)
        state = state * egl[sel] + _bmm(kgt[sel], v_new)
        o_ref[:, c * chunk:(c + 1) * chunk, :] = o.astype(BF16)
    state_ref[...] = state


def gdn_core(p, gb, gbt, rows):
    B, _, S, _ = p.shape
    H = GDN_HEADS

    def slabs(group):
        return pl.BlockSpec((None, H, rows, LANES), lambda b, s: (b, group, s, 0))

    return pl.pallas_call(
        functools.partial(_gdn_core_kernel, rows=rows, chunk=GDN_CHUNK),
        grid=(B, S // rows),
        in_specs=[slabs(0), slabs(1), slabs(2),
                  pl.BlockSpec((None, rows, LANES), lambda b, s: (b, s, 0)),
                  pl.BlockSpec((None, 2 * H, rows), lambda b, s: (b, 0, s))],
        out_specs=pl.BlockSpec((None, H, rows, LANES), lambda b, s: (b, 0, s, 0)),
        out_shape=jax.ShapeDtypeStruct((B, H, S, LANES), BF16),
        scratch_shapes=[pltpu.VMEM((H, GDN_DK, LANES), F32)],
        compiler_params=_cparams(("arbitrary", "arbitrary"), 32 << 20),
        name="gdn_core",
    )(p, p, p, gb, gbt)


def _alibi_slope(head):
    return 2.0 ** (-8.0 * (head + 1) / SWA_Q_HEADS)


def _swa_kernel(sink_ref, qt_ref, kp_ref, kc_ref, vtp_ref, vtc_ref, o_ref, bias_ref, *, layer_slot, n_blocks):
    blk = pl.program_id(1)
    W = SWA_WINDOW
    hd = SWA_HEAD_DIM
    KV = SWA_KV_HEADS
    G = SWA_Q_HEADS // KV
    NEG = -jnp.inf

    @pl.when(jnp.logical_and(pl.program_id(0) == 0, blk == 0))
    def _():
        key = lax.broadcasted_iota(jnp.int32, (2 * W, G * W), 0)
        col = lax.broadcasted_iota(jnp.int32, (2 * W, G * W), 1)
        dist = (col & (W - 1)) + W - key
        valid = (dist >= 0) & (dist < SWA_WINDOW)
        distf = dist.astype(F32)
        for j in range(KV):
            slope = jnp.zeros((2 * W, G * W), F32)
            for a in range(G):
                slope = jnp.where((col >= a * W) & (col < (a + 1) * W), _alibi_slope(G * j + a), slope)
            bias = jnp.where(valid, -slope * distf, NEG)
            bias_ref[1, j] = bias
            bias_ref[0, j] = jnp.where(key >= W, bias, NEG)

    kall = jnp.concatenate([kp_ref[...], kc_ref[...]], axis=0)
    vtall = jnp.concatenate([vtp_ref[...], vtc_ref[...]], axis=1)
    zpad = jnp.zeros((hd, W), BF16)
    ks, ws, vts = [], [], []
    for i in range(n_blocks):
        for j in range(KV):
            tile, half = divmod(j, LANES // hd)
            ks.append(kall[i * W:(i + 2) * W, tile * LANES:(tile + 1) * LANES])
            cols = []
            for a in range(G):
                h = G * j + a
                qt = qt_ref[h * hd:(h + 1) * hd, i * W:(i + 1) * W]
                cols.append(jnp.concatenate([qt, zpad] if half == 0 else [zpad, qt], axis=0))
            ws.append(jnp.concatenate(cols, axis=1))
            vts.append(vtall[j * hd:(j + 1) * hd, i * W:(i + 2) * W])
    ks, ws, vts = jnp.stack(ks), jnp.stack(ws), jnp.stack(vts)

    s = jnp.einsum("jkd,jdq->jkq", ks, ws, preferred_element_type=F32)
    bias = [bias_ref[jnp.minimum(blk, 1)]] + [bias_ref[1]] * (n_blocks - 1)
    s = s + (jnp.concatenate(bias, axis=0) if n_blocks > 1 else bias[0])
    lane = lax.broadcasted_iota(jnp.int32, (1, G * W), 1)
    sinks = []
    for j in range(KV):
        row = jnp.zeros((1, G * W), F32)
        for a in range(G):
            row = jnp.where((lane >= a * W) & (lane < (a + 1) * W), sink_ref[layer_slot, G * j + a], row)
        sinks.append(row)
    sink = jnp.stack(sinks * n_blocks)
    m = jnp.maximum(jnp.max(s, axis=1, keepdims=True), sink)
    p = jnp.exp(s - m)
    denom = jnp.sum(p, axis=1, keepdims=True) + jnp.exp(sink - m)
    ot = jnp.einsum("jdk,jkq->jdq", vts, p.astype(BF16), preferred_element_type=F32)
    ot = ot * (1.0 / denom)
    for i in range(n_blocks):
        for t in range(SWA_Q_HEADS // 2):
            j, a0 = divmod(2 * t, G)
            oj = ot[i * KV + j]
            pair = jnp.concatenate([oj[:, a0 * W:(a0 + 1) * W], oj[:, (a0 + 1) * W:(a0 + 2) * W]], axis=0)
            o_ref[i * W:(i + 1) * W, t * LANES:(t + 1) * LANES] = pair.T.astype(BF16)


def swa_attention(qt, k_sh, vt_sh, sinks, layer_slot):
    B, QW, S = qt.shape
    KW = k_sh.shape[-1]
    W = SWA_WINDOW
    G = SWA_Q_HEADS // SWA_KV_HEADS
    nb = SWA_BLOCKS_PER_STEP if S % (SWA_BLOCKS_PER_STEP * W) == 0 else 1
    T = nb * W
    prev = lambda n: jnp.maximum(n * nb - 1, 0)
    return pl.pallas_call(
        functools.partial(_swa_kernel, layer_slot=layer_slot, n_blocks=nb),
        grid=(B, S // T),
        in_specs=[pl.BlockSpec(memory_space=pltpu.SMEM),
                  pl.BlockSpec((None, QW, T), lambda b, n: (b, 0, n)),
                  pl.BlockSpec((None, W, KW), lambda b, n: (b, prev(n), 0)),
                  pl.BlockSpec((None, T, KW), lambda b, n: (b, n, 0)),
                  pl.BlockSpec((None, KW, W), lambda b, n: (b, 0, prev(n))),
                  pl.BlockSpec((None, KW, T), lambda b, n: (b, 0, n))],
        out_specs=pl.BlockSpec((None, T, QW), lambda b, n: (b, n, 0)),
        out_shape=jax.ShapeDtypeStruct((B, S, QW), BF16),
        scratch_shapes=[pltpu.VMEM((2, SWA_KV_HEADS, 2 * W, G * W), F32)],
        compiler_params=_cparams(("arbitrary", "arbitrary"), (8 + 6 * nb) << 20),
        name="swa_attention",
    )(sinks, qt, k_sh, k_sh, vt_sh, vt_sh)


def _post_mlp_kernel(*refs, gated, tail, ff_chunk, gdn_chunk, n_split):
    n_fixed = 12 if gated else 10
    if gated:
        (x_ref, a_ref, z_ref, onorm_ref, wo_ref, w1_ref, w2_ref, gains_ref,
         gate_mix_ref, shift_mlp_ref, scale_mlp_ref, gate_mlp_ref) = refs[:n_fixed]
    else:
        (x_ref, a_ref, wo_ref, w1_ref, w2_ref, gains_ref,
         gate_mix_ref, shift_mlp_ref, scale_mlp_ref, gate_mlp_ref) = refs[:n_fixed]
    rest = refs[n_fixed:]
    sub = x_ref.shape[0] // n_split
    parts = [slice(i * sub, (i + 1) * sub) for i in range(n_split)]
    xs = [x_ref[sl, :] for sl in parts]
    if gated:
        acts = []
        for sl in parts:
            g = (_rms(a_ref[:, sl, :].astype(F32), NORM_EPS * GDN_DK) * onorm_ref[...]
                 * _silu(z_ref[:, sl, :].astype(F32))).astype(BF16)
            acts.append(jnp.concatenate([g[h] for h in range(g.shape[0])], axis=1))
    else:
        acts = [a_ref[sl, :] for sl in parts]
    ys = [jnp.dot(act, wo_ref[...], preferred_element_type=F32) for act in acts]
    xs = [_adaln_post(x, y, gains_ref[1:2, :], gate_mix_ref[...]) for x, y in zip(xs, ys)]
    hs = [_adaln_pre(x, gains_ref[2:3, :], shift_mlp_ref[...], scale_mlp_ref[...]).astype(BF16) for x in xs]
    d_ff = w1_ref.shape[1]
    accs = [None] * n_split
    for f0 in range(0, d_ff, ff_chunk):
        for i in range(n_split):
            a = jnp.dot(hs[i], w1_ref[:, f0:f0 + ff_chunk], preferred_element_type=F32)
            a = jnp.square(jnp.maximum(a, 0.0)).astype(BF16)
            part = jnp.dot(a, w2_ref[f0:f0 + ff_chunk, :], preferred_element_type=F32)
            accs[i] = part if accs[i] is None else accs[i] + part
    xs = [_adaln_post(x, acc, gains_ref[3:4, :], gate_mlp_ref[...]) for x, acc in zip(xs, accs)]

    def store_x(xo_ref):
        for sl, x in zip(parts, xs):
            xo_ref[sl, :] = x

    def q_proj(ngains_ref, nshift_ref, nscale_ref, wq_ref, q_ref):
        for sl, x in zip(parts, xs):
            hq = _adaln_pre(x, ngains_ref[0:1, :], nshift_ref[...], nscale_ref[...]).astype(BF16)
            q = jnp.dot(hq, wq_ref[...], preferred_element_type=F32) * (SWA_HEAD_DIM ** -0.5)
            q_ref[:, sl] = q.T.astype(BF16)

    if tail == "none":
        (xo_ref,) = rest
        store_x(xo_ref)
    elif tail == "gdn_in":
        assert n_split == 1
        (ngains_ref, nshift_ref, nscale_ref, w_ref, wab_ref, alog_ref, dtb_ref, convw_ref,
         xo_ref, p_ref, gb_ref, gbt_ref, stage_ref, carry_ref) = rest
        store_x(xo_ref)
        _gdn_in_body(xs[0], pl.program_id(1) == 0, ngains_ref[0:1, :], nshift_ref, nscale_ref, w_ref, wab_ref,
                     alog_ref, dtb_ref, convw_ref, p_ref, gb_ref, gbt_ref, stage_ref, carry_ref, gdn_chunk)
    elif tail == "kv_q":
        (kvgain_ref, kvshift_ref, kvscale_ref, wkv_ref, ngains_ref, nshift_ref, nscale_ref, wq_ref,
         xo_ref, k_ref, v_ref, q_ref) = rest
        store_x(xo_ref)
        for sl, x in zip(parts, xs):
            hk = _adaln_pre(x, kvgain_ref[...], kvshift_ref[...], kvscale_ref[...]).astype(BF16)
            kv = jnp.dot(hk, wkv_ref[...], preferred_element_type=F32)
            kvw = kv.shape[1] // 2
            k_ref[sl, :] = kv[:, :kvw].astype(BF16)
            v_ref[:, sl] = kv[:, kvw:].T.astype(BF16)
        q_proj(ngains_ref, nshift_ref, nscale_ref, wq_ref, q_ref)
    elif tail == "q":
        (ngains_ref, nshift_ref, nscale_ref, wq_ref, xo_ref, q_ref) = rest
        store_x(xo_ref)
        q_proj(ngains_ref, nshift_ref, nscale_ref, wq_ref, q_ref)
    else:
        raise ValueError(tail)


def post_mlp(x, act, mods, layer, gains, w_o, wo_idx, w1, w2, tm, *, gate_args=None,
             tail="none", tail_args=None, n_split=1):
    B, S, D = x.shape
    gated = gate_args is not None
    row = lambda width: pl.BlockSpec((None, tm, width), lambda b, s: (b, s, 0))
    col = lambda width: pl.BlockSpec((None, width, tm), lambda b, s: (b, 0, s))
    ins, in_specs = [x], [row(D)]
    if gated:
        z_src, z_group, onorm, oidx = gate_args
        H = act.shape[1]
        ins += [act, z_src, onorm]
        in_specs += [pl.BlockSpec((None, H, tm, LANES), lambda b, s: (b, 0, s, 0)),
                     pl.BlockSpec((None, H, tm, LANES), lambda b, s: (b, z_group, s, 0)),
                     _layer_resident(onorm, oidx)]
    else:
        ins += [act]
        in_specs += [row(act.shape[-1])]
    ins += [w_o, w1, w2, gains, mods, mods, mods, mods]
    in_specs += [_layer_resident(w_o, wo_idx), _layer_resident(w1, layer), _layer_resident(w2, layer),
                 _layer_resident(gains, layer),
                 _mod_spec(D, layer, 2), _mod_spec(D, layer, 3), _mod_spec(D, layer, 4), _mod_spec(D, layer, 5)]
    out_specs = [row(D)]
    out_shape = [jax.ShapeDtypeStruct((B, S, D), F32)]
    scratch = []
    weight_bytes = (w_o[0].size + w1[0].size + w2[0].size) * 2
    extra = 0
    if tail == "gdn_in":
        nlayer, gp, gi = tail_args
        w_ins, w_specs, wb = _gdn_in_inputs(gp, gi)
        ins += [gains, mods, mods] + w_ins
        in_specs += [_layer_resident(gains, nlayer), _mod_spec(D, nlayer, 0), _mod_spec(D, nlayer, 1)] + w_specs
        n_slabs = gp["main"] // LANES
        os_, osh = _gdn_in_outputs(B, S, tm, n_slabs)
        out_specs += os_
        out_shape += osh
        scratch = _gdn_in_scratch(tm)
        weight_bytes += wb
        extra = 2 * tm * n_slabs * LANES * 2 + _gdn_in_scratch_bytes(tm)
    elif tail == "kv_q":
        kvmods, kvgain, w_kv, nlayer, w_q, qi = tail_args
        kv_spec = lambda k: pl.BlockSpec((None, None, 1, D), lambda b, s: (0, b, 0, k))
        ins += [kvgain.reshape(1, D), kvmods, kvmods, w_kv, gains, mods, mods, w_q]
        in_specs += [_resident((1, D)), kv_spec(0), kv_spec(1), _resident(w_kv.shape),
                     _layer_resident(gains, nlayer), _mod_spec(D, nlayer, 0), _mod_spec(D, nlayer, 1),
                     _layer_resident(w_q, qi)]
        kvw = w_kv.shape[1] // 2
        qw = w_q.shape[-1]
        out_specs += [row(kvw), col(kvw), col(qw)]
        out_shape += [jax.ShapeDtypeStruct((B, S, kvw), BF16), jax.ShapeDtypeStruct((B, kvw, S), BF16),
                      jax.ShapeDtypeStruct((B, qw, S), BF16)]
        weight_bytes += (w_kv.size + w_q[0].size) * 2
        extra = 2 * tm * (2 * kvw + qw) * 2
    elif tail == "q":
        nlayer, w_q, qi = tail_args
        qw = w_q.shape[-1]
        ins += [gains, mods, mods, w_q]
        in_specs += [_layer_resident(gains, nlayer), _mod_spec(D, nlayer, 0), _mod_spec(D, nlayer, 1),
                     _layer_resident(w_q, qi)]
        out_specs += [col(qw)]
        out_shape += [jax.ShapeDtypeStruct((B, qw, S), BF16)]
        weight_bytes += w_q[0].size * 2
        extra = 2 * tm * qw * 2
    ff_chunk = 512
    vmem = (weight_bytes + extra + 4 * tm * D * 4 + 4 * tm * D * 2 + 6 * tm * D * 4
            + 3 * tm * ff_chunk * 4 + (4 << 20))
    return pl.pallas_call(
        functools.partial(_post_mlp_kernel, gated=gated, tail=tail, ff_chunk=ff_chunk, gdn_chunk=GDN_CHUNK,
                          n_split=n_split),
        grid=(B, S // tm),
        in_specs=in_specs, out_specs=out_specs, out_shape=out_shape,
        scratch_shapes=scratch,
        compiler_params=_cparams(("arbitrary", "arbitrary"), vmem),
        name="post_mlp_" + tail,
    )(*ins)


def kernel(x, c, mod_w, mod_b, norm_g, gdn_w_in, gdn_conv, gdn_a_log, gdn_dt_bias, gdn_onorm,
           gdn_w_out, kv_mod_w, kv_mod_b, kv_norm, w_kv, swa_w_q, swa_sinks, swa_w_o,
           mlp_w1, mlp_w2):
    B, S, D = x.shape
    depth = mod_w.shape[0]
    n_gdn = gdn_w_in.shape[0]
    tm = min(S, 512)
    tm_wide = min(S, 1024)
    rows = min(S, 256)
    assert S % tm == 0 and S % rows == 0 and rows % GDN_CHUNK == 0 and S % SWA_WINDOW == 0

    mods = modulation(c, mod_w, mod_b).reshape(depth, B, 1, mod_w.shape[-1])
    kvmods = modulation(c, kv_mod_w[None], kv_mod_b[None]).reshape(1, B, 1, kv_mod_w.shape[-1])

    w1 = mlp_w1.astype(BF16)
    w2 = mlp_w2.astype(BF16)
    gp = _gdn_params(gdn_w_in, gdn_conv, gdn_a_log, gdn_dt_bias)
    gdn_wo = gdn_w_out.astype(BF16)
    swa_wq = swa_w_q.astype(BF16)
    swa_wo = swa_w_o.astype(BF16)
    wkv = w_kv.astype(BF16)
    sinks = swa_sinks.astype(F32)
    gains = norm_g.astype(F32)
    onorm = gdn_onorm.astype(F32).reshape(n_gdn, 1, LANES)

    p, gb, gbt = gdn_in_proj(x, mods, 0, gains, gp, 0, tm)
    z_group = 3
    k_sh = v_sh = q = None
    for layer in range(depth):
        nxt = layer + 1
        if layer < n_gdn:
            act = gdn_core(p, gb, gbt, rows)
            w_o, wo_idx, gate_args = gdn_wo, layer, (p, z_group, onorm, layer)
        else:
            act = swa_attention(q, k_sh, v_sh, sinks, layer - n_gdn)
            w_o, wo_idx, gate_args = swa_wo, layer - n_gdn, None
        if nxt == depth:
            tail, tail_args = "none", None
        elif nxt < n_gdn:
            tail, tail_args = "gdn_in", (nxt, gp, nxt)
        elif nxt == n_gdn:
            tail, tail_args = "kv_q", (kvmods, kv_norm, wkv, nxt, swa_wq, 0)
        else:
            tail, tail_args = "q", (nxt, swa_wq, nxt - n_gdn)
        outs = post_mlp(x, act, mods, layer, gains, w_o, wo_idx, w1, w2, tm if gate_args is not None else tm_wide,
                        gate_args=gate_args, tail=tail, tail_args=tail_args,
                        n_split=1 if gate_args is not None else tm_wide // tm)
        x = outs[0]
        if tail == "gdn_in":
            p, gb, gbt = outs[1:]
        elif tail == "kv_q":
            k_sh, v_sh, q = outs[1:]
        elif tail == "q":
            (q,) = outs[1:]
    return x
```

```python
import functools

import jax
import jax.numpy as jnp
from jax import lax
from jax.experimental import pallas as pl
from jax.experimental.pallas import tpu as pltpu

F32 = jnp.float32
BF16 = jnp.bfloat16

NORM_EPS = 1e-6
LOG2_E = 1.4426950408889634
LANES = 128
SUBLANES = 8
GDN_HEADS = 8
GDN_DK = 128
GDN_CONV = 4
SWA_HEAD_DIM = 64
SWA_Q_HEADS = 16
SWA_KV_HEADS = 4
SWA_WINDOW = 128
SWA_BLOCKS_PER_STEP = 8
V7X_SCOPED_VMEM_BYTES = 60000 * 1024

GDN_CHUNK = 128
GDN_SLAB_GROUP = 2
CONV_HALO = SUBLANES


def _cparams(semantics, vmem_bytes):
    return pltpu.CompilerParams(dimension_semantics=semantics,
                                vmem_limit_bytes=min(int(vmem_bytes), V7X_SCOPED_VMEM_BYTES))


def _sigmoid(x):
    return 1.0 / (1.0 + jnp.exp2(x * (-LOG2_E)))


def _silu(x):
    return x * _sigmoid(x)


def _rms(x, eps=NORM_EPS):
    return x * lax.rsqrt(jnp.mean(x * x, axis=-1, keepdims=True) + eps)


def _bdot(a, b):
    return jnp.dot(a.astype(BF16), b.astype(BF16), preferred_element_type=F32)


def _bmm(a, b):
    return jnp.einsum("bij,bjk->bik", a.astype(BF16), b.astype(BF16), preferred_element_type=F32)


def _resident(shape):
    nd = len(shape)
    return pl.BlockSpec(shape, lambda *_: (0,) * nd, pipeline_mode=pl.Buffered(1))


def _layer_resident(arr, layer, block=None):
    block = tuple(arr.shape[1:]) if block is None else tuple(block)
    nd = len(block)
    return pl.BlockSpec((None,) + block, lambda *_: (layer,) + (0,) * nd, pipeline_mode=pl.Buffered(1))


def _mod_kernel(c_ref, w_ref, b_ref, o_ref):
    o_ref[0] = _bdot(_silu(c_ref[...]), w_ref[0]) + b_ref[0]


def modulation(c, w, b):
    L, D, N = w.shape
    B = c.shape[0]
    tn = min(N, 1024)
    assert N % tn == 0
    return pl.pallas_call(
        _mod_kernel,
        grid=(L, N // tn),
        in_specs=[pl.BlockSpec((B, D), lambda l, n: (0, 0)),
                  pl.BlockSpec((1, D, tn), lambda l, n: (l, 0, n)),
                  pl.BlockSpec((1, 1, tn), lambda l, n: (l, 0, n))],
        out_specs=pl.BlockSpec((1, B, tn), lambda l, n: (l, 0, n)),
        out_shape=jax.ShapeDtypeStruct((L, B, N), F32),
        compiler_params=_cparams(("arbitrary", "arbitrary"), 4 * D * tn * 4 + (4 << 20)),
        name="modulation",
    )(c, w, b.reshape(L, 1, N))


def _mod_spec(D, layer, k):
    return pl.BlockSpec((None, None, 1, D), lambda b, *_: (layer, b, 0, k))


def _adaln_pre(x, gain, shift, scale):
    return _rms(x) * (gain * (1.0 + scale)) + shift


def _adaln_post(x, y, gain, gate):
    return x + _rms(y) * ((1.0 + gate) * gain)


def _gdn_gates(ab, alog_ref, dtb_ref, out_gb_ref, out_gbt_ref, chunk):
    tm = ab.shape[0]
    lane = lax.broadcasted_iota(jnp.int32, ab.shape, 1)
    pre = ab + dtb_ref[...]
    softplus = jnp.maximum(pre, 0.0) + jnp.log(1.0 + jnp.exp(-jnp.abs(pre)))
    g = -jnp.exp(alog_ref[...]) * softplus
    g = jnp.where(lane < GDN_HEADS, g, 0.0)
    beta = _sigmoid(ab)
    r = lax.broadcasted_iota(jnp.int32, (chunk, chunk), 0)
    c = lax.broadcasted_iota(jnp.int32, (chunk, chunk), 1)
    tri = (r >= c).astype(F32)
    parts = []
    for i in range(tm // chunk):
        parts.append(jnp.dot(tri, g[i * chunk:(i + 1) * chunk], preferred_element_type=F32,
                             precision=lax.Precision.HIGHEST))
    gcum = jnp.concatenate(parts, axis=0) if len(parts) > 1 else parts[0]
    gb = jnp.where(lane < GDN_HEADS, gcum, jnp.where(lane < 2 * GDN_HEADS, beta, 0.0))
    out_gb_ref[...] = gb
    out_gbt_ref[...] = gb.T[:2 * GDN_HEADS]


def _conv_epilogue(g0, slabs, convw_ref, stage_ref, buf, carry_ref, out_ref):
    group = len(slabs)
    tm = slabs[0].shape[0]
    stage_ref[buf, :, 0:CONV_HALO, :] = carry_ref[g0:g0 + group]
    for j in range(group):
        stage_ref[buf, j, CONV_HALO:CONV_HALO + tm, :] = slabs[j]
    acc = None
    for t in range(GDN_CONV):
        off = CONV_HALO - (GDN_CONV - 1) + t
        term = convw_ref[t, g0:g0 + group] * stage_ref[buf, :, off:off + tm, :]
        acc = term if acc is None else acc + term
    carry_ref[g0:g0 + group] = stage_ref[buf, :, tm:tm + CONV_HALO, :]
    y = _silu(acc)
    if g0 < 2 * GDN_HEADS:
        y = y * lax.rsqrt(jnp.sum(y * y, axis=-1, keepdims=True) + NORM_EPS)
    out_ref[g0:g0 + group] = y.astype(BF16)


def _gdn_group_order(n_slabs):
    group = GDN_SLAB_GROUP
    n_conv = 3 * GDN_HEADS // group
    conv_groups, z_groups = list(range(n_conv)), list(range(n_conv, n_slabs // group))
    per_z = -(-n_conv // max(len(z_groups), 1))
    order = []
    while conv_groups or z_groups:
        order += conv_groups[:per_z]
        conv_groups = conv_groups[per_z:]
        if z_groups:
            order.append(z_groups.pop(0))
    return order


def _gdn_in_body(x, first_row_block, gain, shift_ref, scale_ref, w_ref, wab_ref, alog_ref, dtb_ref,
                 convw_ref, p_ref, gb_ref, gbt_ref, stage_ref, carry_ref, chunk):
    group = GDN_SLAB_GROUP

    @pl.when(first_row_block)
    def _():
        carry_ref[...] = jnp.zeros_like(carry_ref)

    h = _adaln_pre(x, gain, shift_ref[...], scale_ref[...]).astype(BF16)

    def epilogue(gi, res):
        g0 = gi * group
        slabs = [res[:, j * LANES:(j + 1) * LANES] for j in range(group)]
        if g0 >= 3 * GDN_HEADS:
            for j in range(group):
                p_ref[g0 + j] = slabs[j].astype(BF16)
        else:
            _conv_epilogue(g0, slabs, convw_ref, stage_ref, gi % 2, carry_ref, p_ref)

    pending = None
    for gi in _gdn_group_order(w_ref.shape[1] // LANES):
        g0 = gi * group
        res = jnp.dot(h, w_ref[:, g0 * LANES:(g0 + group) * LANES], preferred_element_type=F32)
        if pending is not None:
            epilogue(*pending)
        pending = (gi, res)
    ab = jnp.dot(h, wab_ref[...], preferred_element_type=F32)
    epilogue(*pending)
    _gdn_gates(ab, alog_ref, dtb_ref, gb_ref, gbt_ref, chunk)


def _gdn_in_kernel(x_ref, gains_ref, shift_ref, scale_ref, w_ref, wab_ref, alog_ref, dtb_ref, convw_ref,
                   p_ref, gb_ref, gbt_ref, stage_ref, carry_ref, *, chunk):
    _gdn_in_body(x_ref[...], pl.program_id(1) == 0, gains_ref[0:1, :], shift_ref, scale_ref, w_ref, wab_ref,
                 alog_ref, dtb_ref, convw_ref, p_ref, gb_ref, gbt_ref, stage_ref, carry_ref, chunk)


def _gdn_in_outputs(B, S, tm, n_slabs):
    out_specs = [pl.BlockSpec((None, n_slabs, tm, LANES), lambda b, s: (b, 0, s, 0)),
                 pl.BlockSpec((None, tm, LANES), lambda b, s: (b, s, 0)),
                 pl.BlockSpec((None, 2 * GDN_HEADS, tm), lambda b, s: (b, 0, s))]
    out_shape = [jax.ShapeDtypeStruct((B, n_slabs, S, LANES), BF16),
                 jax.ShapeDtypeStruct((B, S, LANES), F32),
                 jax.ShapeDtypeStruct((B, 2 * GDN_HEADS, S), F32)]
    return out_specs, out_shape


def _gdn_in_scratch(tm):
    return [pltpu.VMEM((2, GDN_SLAB_GROUP, tm + CONV_HALO, LANES), F32),
            pltpu.VMEM((3 * GDN_HEADS, CONV_HALO, LANES), F32)]


def _gdn_in_scratch_bytes(tm):
    return 2 * GDN_SLAB_GROUP * (tm + CONV_HALO) * LANES * 4 + 3 * GDN_HEADS * CONV_HALO * LANES * 4


def _gdn_params(w_in, conv_w, a_log, dt_bias):
    n, D, n_in = w_in.shape
    main = n_in - 2 * GDN_HEADS
    w_all = w_in.astype(BF16)
    w_ab = jnp.zeros((n, D, LANES), F32).at[:, :, :2 * GDN_HEADS].set(w_in[:, :, main:]).astype(BF16)
    alog = jnp.zeros((n, 1, LANES), F32).at[:, 0, :GDN_HEADS].set(a_log.astype(F32))
    dtb = jnp.zeros((n, 1, LANES), F32).at[:, 0, :GDN_HEADS].set(dt_bias.astype(F32))
    convw = conv_w.astype(F32).reshape(n, GDN_CONV, main // LANES - GDN_HEADS, 1, LANES)
    return dict(w_all=w_all, main=main, w_ab=w_ab, alog=alog, dtb=dtb, convw=convw)


def _gdn_in_inputs(gp, i):
    D = gp["w_all"].shape[1]
    ins = [gp["w_all"], gp["w_ab"], gp["alog"], gp["dtb"], gp["convw"]]
    specs = [_layer_resident(gp["w_all"], i, (D, gp["main"])), _layer_resident(gp["w_ab"], i),
             _layer_resident(gp["alog"], i), _layer_resident(gp["dtb"], i), _layer_resident(gp["convw"], i)]
    weight_bytes = (D * gp["main"] + D * LANES) * 2 + gp["convw"][0].size * 4 * SUBLANES
    return ins, specs, weight_bytes


def gdn_in_proj(x, mods, layer, gains, gp, i, tm):
    B, S, D = x.shape
    n_slabs = gp["main"] // LANES
    out_specs, out_shape = _gdn_in_outputs(B, S, tm, n_slabs)
    w_ins, w_specs, weight_bytes = _gdn_in_inputs(gp, i)
    vmem = (2 * tm * D * 4 + weight_bytes + 2 * tm * n_slabs * LANES * 2 + _gdn_in_scratch_bytes(tm)
            + 6 * tm * GDN_SLAB_GROUP * LANES * 4 + (6 << 20))
    return pl.pallas_call(
        functools.partial(_gdn_in_kernel, chunk=GDN_CHUNK),
        grid=(B, S // tm),
        in_specs=[pl.BlockSpec((None, tm, D), lambda b, s: (b, s, 0)),
                  _layer_resident(gains, layer),
                  _mod_spec(D, layer, 0), _mod_spec(D, layer, 1)] + w_specs,
        out_specs=out_specs, out_shape=out_shape,
        scratch_shapes=_gdn_in_scratch(tm),
        compiler_params=_cparams(("arbitrary", "arbitrary"), vmem),
        name="gdn_in_proj",
    )(x, gains, mods, mods, *w_ins)


def _unit_lower_inverse(lmat, ri, ci):
    n = lmat.shape[-1]
    diff = ri ^ ci
    p = jnp.where(ri == ci, 1.0, 0.0) - jnp.where(diff < 2, lmat, 0.0)
    s = 2
    while s < n:
        off = jnp.where(diff >= s, jnp.where(diff < 2 * s, lmat, 0.0), 0.0)
        p = p - _bmm(p, _bmm(off, p))
        s *= 2
    return p


def _gdn_core_kernel(q_ref, k_ref, v_ref, gb_ref, gbt_ref, o_ref, state_ref, *, rows, chunk):
    H = GDN_HEADS
    nc = rows // chunk

    @pl.when(pl.program_id(1) == 0)
    def _():
        state_ref[...] = jnp.zeros_like(state_ref)

    def chunked(t):
        if nc == 1:
            return t
        return jnp.concatenate([t[:, c * chunk:(c + 1) * chunk] for c in range(nc)], axis=0)

    q = chunked(q_ref[...])
    k = chunked(k_ref[...])
    v = chunked(v_ref[...])

    gb = gb_ref[...]
    gbt = gbt_ref[...]
    gcc = chunked(jnp.stack([gb[:, h:h + 1] for h in range(H)]))
    bc = chunked(jnp.stack([gb[:, H + h:H + h + 1] for h in range(H)]))

    def rows_of(base):
        return jnp.stack([gbt[base + h:base + h + 1, c * chunk:(c + 1) * chunk]
                          for c in range(nc) for h in range(H)])

    grc, brc = rows_of(0), rows_of(H)
    g_last = gcc[:, chunk - 1:chunk, :]

    ri = lax.broadcasted_iota(jnp.int32, (1, chunk, chunk), 1)
    ci = lax.broadcasted_iota(jnp.int32, (1, chunk, chunk), 2)
    gcb = jnp.broadcast_to(gcc, (nc * H, chunk, LANES))
    decay = jnp.exp(jnp.where(ri >= ci, gcb[:, :, :chunk] - grc, -jnp.inf))
    kk_qk = jnp.einsum("bcd,bsd->bcs", jnp.concatenate([k, q], axis=1), k,
                       preferred_element_type=F32)
    lmat = jnp.where(ri > ci, kk_qk[:, :chunk] * decay * bc, 0.0)
    attn = (kk_qk[:, chunk:] * decay).astype(BF16)
    tinv = _unit_lower_inverse(lmat, ri, ci)
    u = _bmm(tinv * brc, v)
    w = _bmm(tinv * (brc * jnp.exp(grc)), k)
    wq = jnp.concatenate([w.astype(BF16), q], axis=1)
    eg = jnp.exp(gcb)
    kt = jnp.stack([k[b].astype(F32).T for b in range(nc * H)])
    kgt = (kt * jnp.exp(g_last - grc)).astype(BF16)
    egl = jnp.exp(g_last)

    state = state_ref[...]
    for c in range(nc):
        sel = slice(c * H, (c + 1) * H)
        ws_qs = _bmm(wq[sel], state)
        v_new = (u[sel] - ws_qs[:, :chunk]).astype(BF16)
        o = ws_qs[:, chunk:] * eg[sel] + _bmm(attn[sel], v_new)
        state = state * egl[sel] + _bmm(kgt[sel], v_new)
        o_ref[:, c * chunk:(c + 1) * chunk, :] = o.astype(BF16)
    state_ref[...] = state


def gdn_core(p, gb, gbt, rows):
    B, _, S, _ = p.shape
    H = GDN_HEADS

    def slabs(group):
        return pl.BlockSpec((None, H, rows, LANES), lambda b, s: (b, group, s, 0))

    return pl.pallas_call(
        functools.partial(_gdn_core_kernel, rows=rows, chunk=GDN_CHUNK),
        grid=(B, S // rows),
        in_specs=[slabs(0), slabs(1), slabs(2),
                  pl.BlockSpec((None, rows, LANES), lambda b, s: (b, s, 0)),
                  pl.BlockSpec((None, 2 * H, rows), lambda b, s: (b, 0, s))],
        out_specs=pl.BlockSpec((None, H, rows, LANES), lambda b, s: (b, 0, s, 0)),
        out_shape=jax.ShapeDtypeStruct((B, H, S, LANES), BF16),
        scratch_shapes=[pltpu.VMEM((H, GDN_DK, LANES), F32)],
        compiler_params=_cparams(("arbitrary", "arbitrary"), 32 << 20),
        name="gdn_core",
    )(p, p, p, gb, gbt)


def _alibi_slope(head):
    return 2.0 ** (-8.0 * (head + 1) / SWA_Q_HEADS)


def _swa_kernel(sink_ref, qt_ref, kp_ref, kc_ref, vtp_ref, vtc_ref, o_ref, bias_ref, *, layer_slot, n_blocks):
    blk = pl.program_id(1)
    W = SWA_WINDOW
    hd = SWA_HEAD_DIM
    KV = SWA_KV_HEADS
    G = SWA_Q_HEADS // KV
    NEG = -jnp.inf

    @pl.when(jnp.logical_and(pl.program_id(0) == 0, blk == 0))
    def _():
        key = lax.broadcasted_iota(jnp.int32, (2 * W, G * W), 0)
        col = lax.broadcasted_iota(jnp.int32, (2 * W, G * W), 1)
        dist = (col & (W - 1)) + W - key
        valid = (dist >= 0) & (dist < SWA_WINDOW)
        distf = dist.astype(F32)
        for j in range(KV):
            slope = jnp.zeros((2 * W, G * W), F32)
            for a in range(G):
                slope = jnp.where((col >= a * W) & (col < (a + 1) * W), _alibi_slope(G * j + a), slope)
            bias = jnp.where(valid, -slope * distf, NEG)
            bias_ref[1, j] = bias
            bias_ref[0, j] = jnp.where(key >= W, bias, NEG)

    kall = jnp.concatenate([kp_ref[...], kc_ref[...]], axis=0)
    vtall = jnp.concatenate([vtp_ref[...], vtc_ref[...]], axis=1)
    zpad = jnp.zeros((hd, W), BF16)
    ks, ws, vts = [], [], []
    for i in range(n_blocks):
        for j in range(KV):
            tile, half = divmod(j, LANES // hd)
            ks.append(kall[i * W:(i + 2) * W, tile * LANES:(tile + 1) * LANES])
            cols = []
            for a in range(G):
                h = G * j + a
                qt = qt_ref[h * hd:(h + 1) * hd, i * W:(i + 1) * W]
                cols.append(jnp.concatenate([qt, zpad] if half == 0 else [zpad, qt], axis=0))
            ws.append(jnp.concatenate(cols, axis=1))
            vts.append(vtall[j * hd:(j + 1) * hd, i * W:(i + 2) * W])
    ks, ws, vts = jnp.stack(ks), jnp.stack(ws), jnp.stack(vts)

    s = jnp.einsum("jkd,jdq->jkq", ks, ws, preferred_element_type=F32)
    bias = [bias_ref[jnp.minimum(blk, 1)]] + [bias_ref[1]] * (n_blocks - 1)
    s = s + (jnp.concatenate(bias, axis=0) if n_blocks > 1 else bias[0])
    lane = lax.broadcasted_iota(jnp.int32, (1, G * W), 1)
    sinks = []
    for j in range(KV):
        row = jnp.zeros((1, G * W), F32)
        for a in range(G):
            row = jnp.where((lane >= a * W) & (lane < (a + 1) * W), sink_ref[layer_slot, G * j + a], row)
        sinks.append(row)
    sink = jnp.stack(sinks * n_blocks)
    m = jnp.maximum(jnp.max(s, axis=1, keepdims=True), sink)
    p = jnp.exp(s - m)
    denom = jnp.sum(p, axis=1, keepdims=True) + jnp.exp(sink - m)
    ot = jnp.einsum("jdk,jkq->jdq", vts, p.astype(BF16), preferred_element_type=F32)
    ot = ot * (1.0 / denom)
    for i in range(n_blocks):
        for t in range(SWA_Q_HEADS // 2):
            j, a0 = divmod(2 * t, G)
            oj = ot[i * KV + j]
            pair = jnp.concatenate([oj[:, a0 * W:(a0 + 1) * W], oj[:, (a0 + 1) * W:(a0 + 2) * W]], axis=0)
            o_ref[i * W:(i + 1) * W, t * LANES:(t + 1) * LANES] = pair.T.astype(BF16)


def swa_attention(qt, k_sh, vt_sh, sinks, layer_slot):
    B, QW, S = qt.shape
    KW = k_sh.shape[-1]
    W = SWA_WINDOW
    G = SWA_Q_HEADS // SWA_KV_HEADS
    nb = SWA_BLOCKS_PER_STEP if S % (SWA_BLOCKS_PER_STEP * W) == 0 else 1
    T = nb * W
    prev = lambda n: jnp.maximum(n * nb - 1, 0)
    return pl.pallas_call(
        functools.partial(_swa_kernel, layer_slot=layer_slot, n_blocks=nb),
        grid=(B, S // T),
        in_specs=[pl.BlockSpec(memory_space=pltpu.SMEM),
                  pl.BlockSpec((None, QW, T), lambda b, n: (b, 0, n)),
                  pl.BlockSpec((None, W, KW), lambda b, n: (b, prev(n), 0)),
                  pl.BlockSpec((None, T, KW), lambda b, n: (b, n, 0)),
                  pl.BlockSpec((None, KW, W), lambda b, n: (b, 0, prev(n))),
                  pl.BlockSpec((None, KW, T), lambda b, n: (b, 0, n))],
        out_specs=pl.BlockSpec((None, T, QW), lambda b, n: (b, n, 0)),
        out_shape=jax.ShapeDtypeStruct((B, S, QW), BF16),
        scratch_shapes=[pltpu.VMEM((2, SWA_KV_HEADS, 2 * W, G * W), F32)],
        compiler_params=_cparams(("arbitrary", "arbitrary"), (8 + 6 * nb) << 20),
        name="swa_attention",
    )(sinks, qt, k_sh, k_sh, vt_sh, vt_sh)


def _post_mlp_kernel(*refs, gated, tail, ff_chunk, n_split):
    n_fixed = 12 if gated else 10
    if gated:
        (x_ref, a_ref, z_ref, onorm_ref, wo_ref, w1_ref, w2_ref, gains_ref,
         gate_mix_ref, shift_mlp_ref, scale_mlp_ref, gate_mlp_ref) = refs[:n_fixed]
    else:
        (x_ref, a_ref, wo_ref, w1_ref, w2_ref, gains_ref,
         gate_mix_ref, shift_mlp_ref, scale_mlp_ref, gate_mlp_ref) = refs[:n_fixed]
    rest = refs[n_fixed:]
    sub = x_ref.shape[0] // n_split
    parts = [slice(i * sub, (i + 1) * sub) for i in range(n_split)]
    xs = [x_ref[sl, :] for sl in parts]
    if gated:
        acts = []
        for sl in parts:
            g = (_rms(a_ref[:, sl, :].astype(F32), NORM_EPS * GDN_DK) * onorm_ref[...]
                 * _silu(z_ref[:, sl, :].astype(F32))).astype(BF16)
            acts.append(jnp.concatenate([g[h] for h in range(g.shape[0])], axis=1))
    else:
        acts = [a_ref[sl, :] for sl in parts]
    ys = [jnp.dot(act, wo_ref[...], preferred_element_type=F32) for act in acts]
    xs = [_adaln_post(x, y, gains_ref[1:2, :], gate_mix_ref[...]) for x, y in zip(xs, ys)]
    hs = [_adaln_pre(x, gains_ref[2:3, :], shift_mlp_ref[...], scale_mlp_ref[...]).astype(BF16) for x in xs]
    d_ff = w1_ref.shape[1]
    accs = [None] * n_split
    for f0 in range(0, d_ff, ff_chunk):
        for i in range(n_split):
            a = jnp.dot(hs[i], w1_ref[:, f0:f0 + ff_chunk], preferred_element_type=F32)
            a = jnp.square(jnp.maximum(a, 0.0)).astype(BF16)
            part = jnp.dot(a, w2_ref[f0:f0 + ff_chunk, :], preferred_element_type=F32)
            accs[i] = part if accs[i] is None else accs[i] + part
    xs = [_adaln_post(x, acc, gains_ref[3:4, :], gate_mlp_ref[...]) for x, acc in zip(xs, accs)]

    def store_x(xo_ref):
        for sl, x in zip(parts, xs):
            xo_ref[sl, :] = x

    def q_proj(ngains_ref, nshift_ref, nscale_ref, wq_ref, q_ref):
        for sl, x in zip(parts, xs):
            hq = _adaln_pre(x, ngains_ref[0:1, :], nshift_ref[...], nscale_ref[...]).astype(BF16)
            q = jnp.dot(hq, wq_ref[...], preferred_element_type=F32) * (SWA_HEAD_DIM ** -0.5)
            q_ref[:, sl] = q.T.astype(BF16)

    if tail == "none":
        (xo_ref,) = rest
        store_x(xo_ref)
    elif tail == "kv_q":
        (kvgain_ref, kvshift_ref, kvscale_ref, wkv_ref, ngains_ref, nshift_ref, nscale_ref, wq_ref,
         xo_ref, k_ref, v_ref, q_ref) = rest
        store_x(xo_ref)
        for sl, x in zip(parts, xs):
            hk = _adaln_pre(x, kvgain_ref[...], kvshift_ref[...], kvscale_ref[...]).astype(BF16)
            kv = jnp.dot(hk, wkv_ref[...], preferred_element_type=F32)
            kvw = kv.shape[1] // 2
            k_ref[sl, :] = kv[:, :kvw].astype(BF16)
            v_ref[:, sl] = kv[:, kvw:].T.astype(BF16)
        q_proj(ngains_ref, nshift_ref, nscale_ref, wq_ref, q_ref)
    elif tail == "q":
        (ngains_ref, nshift_ref, nscale_ref, wq_ref, xo_ref, q_ref) = rest
        store_x(xo_ref)
        q_proj(ngains_ref, nshift_ref, nscale_ref, wq_ref, q_ref)
    else:
        raise ValueError(tail)


def post_mlp(x, act, mods, layer, gains, w_o, wo_idx, w1, w2, tm, *, gate_args=None,
             tail="none", tail_args=None, n_split=1):
    B, S, D = x.shape
    gated = gate_args is not None
    row = lambda width: pl.BlockSpec((None, tm, width), lambda b, s: (b, s, 0))
    col = lambda width: pl.BlockSpec((None, width, tm), lambda b, s: (b, 0, s))
    ins, in_specs = [x], [row(D)]
    if gated:
        z_src, z_group, onorm, oidx = gate_args
        H = act.shape[1]
        ins += [act, z_src, onorm]
        in_specs += [pl.BlockSpec((None, H, tm, LANES), lambda b, s: (b, 0, s, 0)),
                     pl.BlockSpec((None, H, tm, LANES), lambda b, s: (b, z_group, s, 0)),
                     _layer_resident(onorm, oidx)]
    else:
        ins += [act]
        in_specs += [row(act.shape[-1])]
    ins += [w_o, w1, w2, gains, mods, mods, mods, mods]
    in_specs += [_layer_resident(w_o, wo_idx), _layer_resident(w1, layer), _layer_resident(w2, layer),
                 _layer_resident(gains, layer),
                 _mod_spec(D, layer, 2), _mod_spec(D, layer, 3), _mod_spec(D, layer, 4), _mod_spec(D, layer, 5)]
    out_specs = [row(D)]
    out_shape = [jax.ShapeDtypeStruct((B, S, D), F32)]
    weight_bytes = (w_o[0].size + w1[0].size + w2[0].size) * 2
    extra = 0
    if tail == "kv_q":
        kvmods, kvgain, w_kv, nlayer, w_q, qi = tail_args
        kv_spec = lambda k: pl.BlockSpec((None, None, 1, D), lambda b, s: (0, b, 0, k))
        ins += [kvgain.reshape(1, D), kvmods, kvmods, w_kv, gains, mods, mods, w_q]
        in_specs += [_resident((1, D)), kv_spec(0), kv_spec(1), _resident(w_kv.shape),
                     _layer_resident(gains, nlayer), _mod_spec(D, nlayer, 0), _mod_spec(D, nlayer, 1),
                     _layer_resident(w_q, qi)]
        kvw = w_kv.shape[1] // 2
        qw = w_q.shape[-1]
        out_specs += [row(kvw), col(kvw), col(qw)]
        out_shape += [jax.ShapeDtypeStruct((B, S, kvw), BF16), jax.ShapeDtypeStruct((B, kvw, S), BF16),
                      jax.ShapeDtypeStruct((B, qw, S), BF16)]
        weight_bytes += (w_kv.size + w_q[0].size) * 2
        extra = 2 * tm * (2 * kvw + qw) * 2
    elif tail == "q":
        nlayer, w_q, qi = tail_args
        qw = w_q.shape[-1]
        ins += [gains, mods, mods, w_q]
        in_specs += [_layer_resident(gains, nlayer), _mod_spec(D, nlayer, 0), _mod_spec(D, nlayer, 1),
                     _layer_resident(w_q, qi)]
        out_specs += [col(qw)]
        out_shape += [jax.ShapeDtypeStruct((B, qw, S), BF16)]
        weight_bytes += w_q[0].size * 2
        extra = 2 * tm * qw * 2
    ff_chunk = 512
    vmem = (weight_bytes + extra + 4 * tm * D * 4 + 4 * tm * D * 2 + 6 * tm * D * 4
            + 3 * tm * ff_chunk * 4 + (4 << 20))
    return pl.pallas_call(
        functools.partial(_post_mlp_kernel, gated=gated, tail=tail, ff_chunk=ff_chunk, n_split=n_split),
        grid=(B, S // tm),
        in_specs=in_specs, out_specs=out_specs, out_shape=out_shape,
        compiler_params=_cparams(("arbitrary", "arbitrary"), vmem),
        name="post_mlp_" + tail,
    )(*ins)


def kernel(x, c, mod_w, mod_b, norm_g, gdn_w_in, gdn_conv, gdn_a_log, gdn_dt_bias, gdn_onorm,
           gdn_w_out, kv_mod_w, kv_mod_b, kv_norm, w_kv, swa_w_q, swa_sinks, swa_w_o,
           mlp_w1, mlp_w2):
    B, S, D = x.shape
    depth = mod_w.shape[0]
    n_gdn = gdn_w_in.shape[0]
    tm = min(S, 512)
    tm_wide = min(S, 1024)
    rows = min(S, 256)
    assert S % tm == 0 and S % rows == 0 and rows % GDN_CHUNK == 0 and S % SWA_WINDOW == 0

    mods = modulation(c, mod_w, mod_b).reshape(depth, B, 1, mod_w.shape[-1])
    kvmods = modulation(c, kv_mod_w[None], kv_mod_b[None]).reshape(1, B, 1, kv_mod_w.shape[-1])

    w1 = mlp_w1.astype(BF16)
    w2 = mlp_w2.astype(BF16)
    gp = _gdn_params(gdn_w_in, gdn_conv, gdn_a_log, gdn_dt_bias)
    gdn_wo = gdn_w_out.astype(BF16)
    swa_wq = swa_w_q.astype(BF16)
    swa_wo = swa_w_o.astype(BF16)
    wkv = w_kv.astype(BF16)
    sinks = swa_sinks.astype(F32)
    gains = norm_g.astype(F32)
    onorm = gdn_onorm.astype(F32).reshape(n_gdn, 1, LANES)

    p, gb, gbt = gdn_in_proj(x, mods, 0, gains, gp, 0, tm)
    z_group = 3
    k_sh = v_sh = q = None
    for layer in range(depth):
        nxt = layer + 1
        if layer < n_gdn:
            act = gdn_core(p, gb, gbt, rows)
            w_o, wo_idx, gate_args = gdn_wo, layer, (p, z_group, onorm, layer)
        else:
            act = swa_attention(q, k_sh, v_sh, sinks, layer - n_gdn)
            w_o, wo_idx, gate_args = swa_wo, layer - n_gdn, None
        if nxt == depth:
            tail, tail_args = "none", None
        elif nxt < n_gdn:
            tail, tail_args = "none", None
        elif nxt == n_gdn:
            tail, tail_args = "kv_q", (kvmods, kv_norm, wkv, nxt, swa_wq, 0)
        else:
            tail, tail_args = "q", (nxt, swa_wq, nxt - n_gdn)
        wide = gate_args is None or tail == "none"
        outs = post_mlp(x, act, mods, layer, gains, w_o, wo_idx, w1, w2, tm_wide if wide else tm,
                        gate_args=gate_args, tail=tail, tail_args=tail_args,
                        n_split=tm_wide // tm if wide else 1)
        x = outs[0]
        if nxt < n_gdn:
            p, gb, gbt = gdn_in_proj(x, mods, nxt, gains, gp, nxt, tm)
        elif tail == "kv_q":
            k_sh, v_sh, q = outs[1:]
        elif tail == "q":
            (q,) = outs[1:]
    return x
```

```python
import functools

import jax
import jax.numpy as jnp
from jax import lax
from jax.experimental import pallas as pl
from jax.experimental.pallas import tpu as pltpu

F32 = jnp.float32
BF16 = jnp.bfloat16

NORM_EPS = 1e-6
LOG2_E = 1.4426950408889634
LANES = 128
SUBLANES = 8
GDN_HEADS = 8
GDN_DK = 128
GDN_CONV = 4
SWA_HEAD_DIM = 64
SWA_Q_HEADS = 16
SWA_KV_HEADS = 4
SWA_WINDOW = 128
SWA_BLOCKS_PER_STEP = 8
V7X_SCOPED_VMEM_BYTES = 60000 * 1024

GDN_CHUNK = 128
GDN_SLAB_GROUP = 2
CONV_HALO = SUBLANES


def _cparams(semantics, vmem_bytes):
    return pltpu.CompilerParams(dimension_semantics=semantics,
                                vmem_limit_bytes=min(int(vmem_bytes), V7X_SCOPED_VMEM_BYTES))


def _sigmoid(x):
    return 1.0 / (1.0 + jnp.exp2(x * (-LOG2_E)))


def _silu(x):
    return x * _sigmoid(x)


def _rms(x, eps=NORM_EPS):
    return x * lax.rsqrt(jnp.mean(x * x, axis=-1, keepdims=True) + eps)


def _bdot(a, b):
    return jnp.dot(a.astype(BF16), b.astype(BF16), preferred_element_type=F32)


def _bmm(a, b):
    return jnp.einsum("bij,bjk->bik", a.astype(BF16), b.astype(BF16), preferred_element_type=F32)


def _resident(shape):
    nd = len(shape)
    return pl.BlockSpec(shape, lambda *_: (0,) * nd, pipeline_mode=pl.Buffered(1))


def _layer_resident(arr, layer, block=None):
    block = tuple(arr.shape[1:]) if block is None else tuple(block)
    nd = len(block)
    return pl.BlockSpec((None,) + block, lambda *_: (layer,) + (0,) * nd, pipeline_mode=pl.Buffered(1))


def _mod_kernel(c_ref, w_ref, b_ref, o_ref):
    o_ref[0] = _bdot(_silu(c_ref[...]), w_ref[0]) + b_ref[0]


def modulation(c, w, b):
    L, D, N = w.shape
    B = c.shape[0]
    tn = min(N, 1024)
    assert N % tn == 0
    return pl.pallas_call(
        _mod_kernel,
        grid=(L, N // tn),
        in_specs=[pl.BlockSpec((B, D), lambda l, n: (0, 0)),
                  pl.BlockSpec((1, D, tn), lambda l, n: (l, 0, n)),
                  pl.BlockSpec((1, 1, tn), lambda l, n: (l, 0, n))],
        out_specs=pl.BlockSpec((1, B, tn), lambda l, n: (l, 0, n)),
        out_shape=jax.ShapeDtypeStruct((L, B, N), F32),
        compiler_params=_cparams(("arbitrary", "arbitrary"), 4 * D * tn * 4 + (4 << 20)),
        name="modulation",
    )(c, w, b.reshape(L, 1, N))


def _mod_spec(D, layer, k):
    return pl.BlockSpec((None, None, 1, D), lambda b, *_: (layer, b, 0, k))


def _adaln_pre(x, gain, shift, scale):
    return _rms(x) * (gain * (1.0 + scale)) + shift


def _adaln_post(x, y, gain, gate):
    return x + _rms(y) * ((1.0 + gate) * gain)


def _gdn_gates(ab, alog_ref, dtb_ref, out_gb_ref, out_gbt_ref, chunk):
    tm = ab.shape[0]
    lane = lax.broadcasted_iota(jnp.int32, ab.shape, 1)
    pre = ab + dtb_ref[...]
    softplus = jnp.maximum(pre, 0.0) + jnp.log(1.0 + jnp.exp(-jnp.abs(pre)))
    g = -jnp.exp(alog_ref[...]) * softplus
    g = jnp.where(lane < GDN_HEADS, g, 0.0)
    beta = _sigmoid(ab)
    r = lax.broadcasted_iota(jnp.int32, (chunk, chunk), 0)
    c = lax.broadcasted_iota(jnp.int32, (chunk, chunk), 1)
    tri = (r >= c).astype(F32)
    parts = []
    for i in range(tm // chunk):
        parts.append(jnp.dot(tri, g[i * chunk:(i + 1) * chunk], preferred_element_type=F32,
                             precision=lax.Precision.HIGHEST))
    gcum = jnp.concatenate(parts, axis=0) if len(parts) > 1 else parts[0]
    gb = jnp.where(lane < GDN_HEADS, gcum, jnp.where(lane < 2 * GDN_HEADS, beta, 0.0))
    out_gb_ref[...] = gb
    out_gbt_ref[...] = gb.T[:2 * GDN_HEADS]


def _conv_epilogue(g0, slabs, convw_ref, stage_ref, buf, carry_ref, out_ref):
    group = len(slabs)
    tm = slabs[0].shape[0]
    stage_ref[buf, :, 0:CONV_HALO, :] = carry_ref[g0:g0 + group]
    for j in range(group):
        stage_ref[buf, j, CONV_HALO:CONV_HALO + tm, :] = slabs[j]
    half = None
    for t in range(GDN_CONV):
        off = CONV_HALO - (GDN_CONV - 1) + t
        term = (0.5 * convw_ref[t, g0:g0 + group]) * stage_ref[buf, :, off:off + tm, :]
        half = term if half is None else half + term
    carry_ref[g0:g0 + group] = stage_ref[buf, :, tm:tm + CONV_HALO, :]
    y = half + half * jnp.tanh(half)
    if g0 < 2 * GDN_HEADS:
        y = y * lax.rsqrt(jnp.sum(y * y, axis=-1, keepdims=True) + NORM_EPS)
    out_ref[g0:g0 + group] = y.astype(BF16)


def _gdn_group_order(n_slabs):
    group = GDN_SLAB_GROUP
    n_conv = 3 * GDN_HEADS // group
    conv_groups, z_groups = list(range(n_conv)), list(range(n_conv, n_slabs // group))
    per_z = -(-n_conv // max(len(z_groups), 1))
    order = []
    while conv_groups or z_groups:
        order += conv_groups[:per_z]
        conv_groups = conv_groups[per_z:]
        if z_groups:
            order.append(z_groups.pop(0))
    return order


def _gdn_in_body(x, first_row_block, gain, shift_ref, scale_ref, w_ref, wab_ref, alog_ref, dtb_ref,
                 convw_ref, p_ref, gb_ref, gbt_ref, stage_ref, carry_ref, chunk):
    group = GDN_SLAB_GROUP

    @pl.when(first_row_block)
    def _():
        carry_ref[...] = jnp.zeros_like(carry_ref)

    h = _adaln_pre(x, gain, shift_ref[...], scale_ref[...]).astype(BF16)

    def epilogue(gi, res):
        g0 = gi * group
        slabs = [res[:, j * LANES:(j + 1) * LANES] for j in range(group)]
        if g0 >= 3 * GDN_HEADS:
            for j in range(group):
                p_ref[g0 + j] = slabs[j].astype(BF16)
        else:
            _conv_epilogue(g0, slabs, convw_ref, stage_ref, gi % 2, carry_ref, p_ref)

    pending = None
    for gi in _gdn_group_order(w_ref.shape[1] // LANES):
        g0 = gi * group
        res = jnp.dot(h, w_ref[:, g0 * LANES:(g0 + group) * LANES], preferred_element_type=F32)
        if pending is not None:
            epilogue(*pending)
        pending = (gi, res)
    ab = jnp.dot(h, wab_ref[...], preferred_element_type=F32)
    epilogue(*pending)
    _gdn_gates(ab, alog_ref, dtb_ref, gb_ref, gbt_ref, chunk)


def _gdn_in_kernel(x_ref, gains_ref, shift_ref, scale_ref, w_ref, wab_ref, alog_ref, dtb_ref, convw_ref,
                   p_ref, gb_ref, gbt_ref, stage_ref, carry_ref, *, chunk):
    _gdn_in_body(x_ref[...], pl.program_id(1) == 0, gains_ref[0:1, :], shift_ref, scale_ref, w_ref, wab_ref,
                 alog_ref, dtb_ref, convw_ref, p_ref, gb_ref, gbt_ref, stage_ref, carry_ref, chunk)


def _gdn_in_outputs(B, S, tm, n_slabs):
    out_specs = [pl.BlockSpec((None, n_slabs, tm, LANES), lambda b, s: (b, 0, s, 0)),
                 pl.BlockSpec((None, tm, LANES), lambda b, s: (b, s, 0)),
                 pl.BlockSpec((None, 2 * GDN_HEADS, tm), lambda b, s: (b, 0, s))]
    out_shape = [jax.ShapeDtypeStruct((B, n_slabs, S, LANES), BF16),
                 jax.ShapeDtypeStruct((B, S, LANES), F32),
                 jax.ShapeDtypeStruct((B, 2 * GDN_HEADS, S), F32)]
    return out_specs, out_shape


def _gdn_in_scratch(tm):
    return [pltpu.VMEM((2, GDN_SLAB_GROUP, tm + CONV_HALO, LANES), F32),
            pltpu.VMEM((3 * GDN_HEADS, CONV_HALO, LANES), F32)]


def _gdn_in_scratch_bytes(tm):
    return 2 * GDN_SLAB_GROUP * (tm + CONV_HALO) * LANES * 4 + 3 * GDN_HEADS * CONV_HALO * LANES * 4


def _gdn_params(w_in, conv_w, a_log, dt_bias):
    n, D, n_in = w_in.shape
    main = n_in - 2 * GDN_HEADS
    w_all = w_in.astype(BF16)
    w_ab = jnp.zeros((n, D, LANES), F32).at[:, :, :2 * GDN_HEADS].set(w_in[:, :, main:]).astype(BF16)
    alog = jnp.zeros((n, 1, LANES), F32).at[:, 0, :GDN_HEADS].set(a_log.astype(F32))
    dtb = jnp.zeros((n, 1, LANES), F32).at[:, 0, :GDN_HEADS].set(dt_bias.astype(F32))
    convw = conv_w.astype(F32).reshape(n, GDN_CONV, main // LANES - GDN_HEADS, 1, LANES)
    return dict(w_all=w_all, main=main, w_ab=w_ab, alog=alog, dtb=dtb, convw=convw)


def _gdn_in_inputs(gp, i):
    D = gp["w_all"].shape[1]
    ins = [gp["w_all"], gp["w_ab"], gp["alog"], gp["dtb"], gp["convw"]]
    specs = [_layer_resident(gp["w_all"], i, (D, gp["main"])), _layer_resident(gp["w_ab"], i),
             _layer_resident(gp["alog"], i), _layer_resident(gp["dtb"], i), _layer_resident(gp["convw"], i)]
    weight_bytes = (D * gp["main"] + D * LANES) * 2 + gp["convw"][0].size * 4 * SUBLANES
    return ins, specs, weight_bytes


def gdn_in_proj(x, mods, layer, gains, gp, i, tm):
    B, S, D = x.shape
    n_slabs = gp["main"] // LANES
    out_specs, out_shape = _gdn_in_outputs(B, S, tm, n_slabs)
    w_ins, w_specs, weight_bytes = _gdn_in_inputs(gp, i)
    vmem = (2 * tm * D * 4 + weight_bytes + 2 * tm * n_slabs * LANES * 2 + _gdn_in_scratch_bytes(tm)
            + 6 * tm * GDN_SLAB_GROUP * LANES * 4 + (6 << 20))
    return pl.pallas_call(
        functools.partial(_gdn_in_kernel, chunk=GDN_CHUNK),
        grid=(B, S // tm),
        in_specs=[pl.BlockSpec((None, tm, D), lambda b, s: (b, s, 0)),
                  _layer_resident(gains, layer),
                  _mod_spec(D, layer, 0), _mod_spec(D, layer, 1)] + w_specs,
        out_specs=out_specs, out_shape=out_shape,
        scratch_shapes=_gdn_in_scratch(tm),
        compiler_params=_cparams(("arbitrary", "arbitrary"), vmem),
        name="gdn_in_proj",
    )(x, gains, mods, mods, *w_ins)


def _unit_lower_inverse(lmat, ri, ci):
    n = lmat.shape[-1]
    diff = ri ^ ci
    p = jnp.where(ri == ci, 1.0, 0.0) - jnp.where(diff < 2, lmat, 0.0)
    s = 2
    while s < n:
        off = jnp.where(diff >= s, jnp.where(diff < 2 * s, lmat, 0.0), 0.0)
        p = p - _bmm(p, _bmm(off, p))
        s *= 2
    return p


def _gdn_core_kernel(q_ref, k_ref, v_ref, gb_ref, gbt_ref, o_ref, state_ref, *, rows, chunk):
    H = GDN_HEADS
    nc = rows // chunk

    @pl.when(pl.program_id(1) == 0)
    def _():
        state_ref[...] = jnp.zeros_like(state_ref)

    def chunked(t):
        if nc == 1:
            return t
        return jnp.concatenate([t[:, c * chunk:(c + 1) * chunk] for c in range(nc)], axis=0)

    q = chunked(q_ref[...])
    k = chunked(k_ref[...])
    v = chunked(v_ref[...])

    gb = gb_ref[...]
    gbt = gbt_ref[...]
    gcc = chunked(jnp.stack([gb[:, h:h + 1] for h in range(H)]))
    bc = chunked(jnp.stack([gb[:, H + h:H + h + 1] for h in range(H)]))

    def rows_of(base):
        return jnp.stack([gbt[base + h:base + h + 1, c * chunk:(c + 1) * chunk]
                          for c in range(nc) for h in range(H)])

    grc, brc = rows_of(0), rows_of(H)
    g_last = gcc[:, chunk - 1:chunk, :]

    ri = lax.broadcasted_iota(jnp.int32, (1, chunk, chunk), 1)
    ci = lax.broadcasted_iota(jnp.int32, (1, chunk, chunk), 2)
    gcb = jnp.broadcast_to(gcc, (nc * H, chunk, LANES))
    decay = jnp.exp(jnp.where(ri >= ci, gcb[:, :, :chunk] - grc, -jnp.inf))
    kk_qk = jnp.einsum("bcd,bsd->bcs", jnp.concatenate([k, q], axis=1), k,
                       preferred_element_type=F32)
    lmat = jnp.where(ri > ci, kk_qk[:, :chunk] * decay * bc, 0.0)
    attn = (kk_qk[:, chunk:] * decay).astype(BF16)
    tinv = _unit_lower_inverse(lmat, ri, ci)
    u = _bmm(tinv * brc, v)
    w = _bmm(tinv * (brc * jnp.exp(grc)), k)
    wq = jnp.concatenate([w.astype(BF16), q], axis=1)
    eg = jnp.exp(gcb)
    kt = jnp.stack([k[b].astype(F32).T for b in range(nc * H)])
    kgt = (kt * jnp.exp(g_last - grc)).astype(BF16)
    egl = jnp.exp(g_last)

    state = state_ref[...]
    for c in range(nc):
        sel = slice(c * H, (c + 1) * H)
        ws_qs = _bmm(wq[sel], state)
        v_new = (u[sel] - ws_qs[:, :chunk]).astype(BF16)
        o = ws_qs[:, chunk:] * eg[sel] + _bmm(attn[sel], v_new)
        state = state * egl[sel] + _bmm(kgt[sel], v_new)
        o_ref[:, c * chunk:(c + 1) * chunk, :] = o.astype(BF16)
    state_ref[...] = state


def gdn_core(p, gb, gbt, rows):
    B, _, S, _ = p.shape
    H = GDN_HEADS

    def slabs(group):
        return pl.BlockSpec((None, H, rows, LANES), lambda b, s: (b, group, s, 0))

    return pl.pallas_call(
        functools.partial(_gdn_core_kernel, rows=rows, chunk=GDN_CHUNK),
        grid=(B, S // rows),
        in_specs=[slabs(0), slabs(1), slabs(2),
                  pl.BlockSpec((None, rows, LANES), lambda b, s: (b, s, 0)),
                  pl.BlockSpec((None, 2 * H, rows), lambda b, s: (b, 0, s))],
        out_specs=pl.BlockSpec((None, H, rows, LANES), lambda b, s: (b, 0, s, 0)),
        out_shape=jax.ShapeDtypeStruct((B, H, S, LANES), BF16),
        scratch_shapes=[pltpu.VMEM((H, GDN_DK, LANES), F32)],
        compiler_params=_cparams(("arbitrary", "arbitrary"), 32 << 20),
        name="gdn_core",
    )(p, p, p, gb, gbt)


def _alibi_slope(head):
    return 2.0 ** (-8.0 * (head + 1) / SWA_Q_HEADS)


def _swa_kernel(sink_ref, qt_ref, kp_ref, kc_ref, vtp_ref, vtc_ref, o_ref, bias_ref, *, layer_slot, n_blocks):
    blk = pl.program_id(1)
    W = SWA_WINDOW
    hd = SWA_HEAD_DIM
    KV = SWA_KV_HEADS
    G = SWA_Q_HEADS // KV
    NEG = -jnp.inf

    @pl.when(jnp.logical_and(pl.program_id(0) == 0, blk == 0))
    def _():
        key = lax.broadcasted_iota(jnp.int32, (2 * W, G * W), 0)
        col = lax.broadcasted_iota(jnp.int32, (2 * W, G * W), 1)
        dist = (col & (W - 1)) + W - key
        valid = (dist >= 0) & (dist < SWA_WINDOW)
        distf = dist.astype(F32)
        for j in range(KV):
            slope = jnp.zeros((2 * W, G * W), F32)
            for a in range(G):
                slope = jnp.where((col >= a * W) & (col < (a + 1) * W), _alibi_slope(G * j + a), slope)
            bias = jnp.where(valid, -slope * distf, NEG)
            bias_ref[1, j] = bias
            bias_ref[0, j] = jnp.where(key >= W, bias, NEG)

    kall = jnp.concatenate([kp_ref[...], kc_ref[...]], axis=0)
    vtall = jnp.concatenate([vtp_ref[...], vtc_ref[...]], axis=1)
    zpad = jnp.zeros((hd, W), BF16)
    ks, ws, vts = [], [], []
    for i in range(n_blocks):
        for j in range(KV):
            tile, half = divmod(j, LANES // hd)
            ks.append(kall[i * W:(i + 2) * W, tile * LANES:(tile + 1) * LANES])
            cols = []
            for a in range(G):
                h = G * j + a
                qt = qt_ref[h * hd:(h + 1) * hd, i * W:(i + 1) * W]
                cols.append(jnp.concatenate([qt, zpad] if half == 0 else [zpad, qt], axis=0))
            ws.append(jnp.concatenate(cols, axis=1))
            vts.append(vtall[j * hd:(j + 1) * hd, i * W:(i + 2) * W])
    ks, ws, vts = jnp.stack(ks), jnp.stack(ws), jnp.stack(vts)

    s = jnp.einsum("jkd,jdq->jkq", ks, ws, preferred_element_type=F32)
    bias = [bias_ref[jnp.minimum(blk, 1)]] + [bias_ref[1]] * (n_blocks - 1)
    s = s + (jnp.concatenate(bias, axis=0) if n_blocks > 1 else bias[0])
    lane = lax.broadcasted_iota(jnp.int32, (1, G * W), 1)
    sinks = []
    for j in range(KV):
        row = jnp.zeros((1, G * W), F32)
        for a in range(G):
            row = jnp.where((lane >= a * W) & (lane < (a + 1) * W), sink_ref[layer_slot, G * j + a], row)
        sinks.append(row)
    sink = jnp.stack(sinks * n_blocks)
    m = jnp.maximum(jnp.max(s, axis=1, keepdims=True), sink)
    p = jnp.exp(s - m)
    denom = jnp.sum(p, axis=1, keepdims=True) + jnp.exp(sink - m)
    ot = jnp.einsum("jdk,jkq->jdq", vts, p.astype(BF16), preferred_element_type=F32)
    ot = ot * (1.0 / denom)
    for i in range(n_blocks):
        for t in range(SWA_Q_HEADS // 2):
            j, a0 = divmod(2 * t, G)
            oj = ot[i * KV + j]
            pair = jnp.concatenate([oj[:, a0 * W:(a0 + 1) * W], oj[:, (a0 + 1) * W:(a0 + 2) * W]], axis=0)
            o_ref[i * W:(i + 1) * W, t * LANES:(t + 1) * LANES] = pair.T.astype(BF16)


def swa_attention(qt, k_sh, vt_sh, sinks, layer_slot):
    B, QW, S = qt.shape
    KW = k_sh.shape[-1]
    W = SWA_WINDOW
    G = SWA_Q_HEADS // SWA_KV_HEADS
    nb = SWA_BLOCKS_PER_STEP if S % (SWA_BLOCKS_PER_STEP * W) == 0 else 1
    T = nb * W
    prev = lambda n: jnp.maximum(n * nb - 1, 0)
    return pl.pallas_call(
        functools.partial(_swa_kernel, layer_slot=layer_slot, n_blocks=nb),
        grid=(B, S // T),
        in_specs=[pl.BlockSpec(memory_space=pltpu.SMEM),
                  pl.BlockSpec((None, QW, T), lambda b, n: (b, 0, n)),
                  pl.BlockSpec((None, W, KW), lambda b, n: (b, prev(n), 0)),
                  pl.BlockSpec((None, T, KW), lambda b, n: (b, n, 0)),
                  pl.BlockSpec((None, KW, W), lambda b, n: (b, 0, prev(n))),
                  pl.BlockSpec((None, KW, T), lambda b, n: (b, 0, n))],
        out_specs=pl.BlockSpec((None, T, QW), lambda b, n: (b, n, 0)),
        out_shape=jax.ShapeDtypeStruct((B, S, QW), BF16),
        scratch_shapes=[pltpu.VMEM((2, SWA_KV_HEADS, 2 * W, G * W), F32)],
        compiler_params=_cparams(("arbitrary", "arbitrary"), (8 + 6 * nb) << 20),
        name="swa_attention",
    )(sinks, qt, k_sh, k_sh, vt_sh, vt_sh)


def _post_mlp_kernel(*refs, gated, tail, ff_chunk, n_split):
    n_fixed = 12 if gated else 10
    if gated:
        (x_ref, a_ref, z_ref, onorm_ref, wo_ref, w1_ref, w2_ref, gains_ref,
         gate_mix_ref, shift_mlp_ref, scale_mlp_ref, gate_mlp_ref) = refs[:n_fixed]
    else:
        (x_ref, a_ref, wo_ref, w1_ref, w2_ref, gains_ref,
         gate_mix_ref, shift_mlp_ref, scale_mlp_ref, gate_mlp_ref) = refs[:n_fixed]
    rest = refs[n_fixed:]
    sub = x_ref.shape[0] // n_split
    parts = [slice(i * sub, (i + 1) * sub) for i in range(n_split)]
    xs = [x_ref[sl, :] for sl in parts]
    if gated:
        acts = []
        for sl in parts:
            g = (_rms(a_ref[:, sl, :].astype(F32), NORM_EPS * GDN_DK) * onorm_ref[...]
                 * _silu(z_ref[:, sl, :].astype(F32))).astype(BF16)
            acts.append(jnp.concatenate([g[h] for h in range(g.shape[0])], axis=1))
    else:
        acts = [a_ref[sl, :] for sl in parts]
    ys = [jnp.dot(act, wo_ref[...], preferred_element_type=F32) for act in acts]
    xs = [_adaln_post(x, y, gains_ref[1:2, :], gate_mix_ref[...]) for x, y in zip(xs, ys)]
    hs = [_adaln_pre(x, gains_ref[2:3, :], shift_mlp_ref[...], scale_mlp_ref[...]).astype(BF16) for x in xs]
    d_ff = w1_ref.shape[1]
    accs = [None] * n_split
    for f0 in range(0, d_ff, ff_chunk):
        for i in range(n_split):
            a = jnp.dot(hs[i], w1_ref[:, f0:f0 + ff_chunk], preferred_element_type=F32)
            a = jnp.square(jnp.maximum(a, 0.0)).astype(BF16)
            part = jnp.dot(a, w2_ref[f0:f0 + ff_chunk, :], preferred_element_type=F32)
            accs[i] = part if accs[i] is None else accs[i] + part
    xs = [_adaln_post(x, acc, gains_ref[3:4, :], gate_mlp_ref[...]) for x, acc in zip(xs, accs)]

    def store_x(xo_ref):
        for sl, x in zip(parts, xs):
            xo_ref[sl, :] = x

    def q_proj(ngains_ref, nshift_ref, nscale_ref, wq_ref, q_ref):
        for sl, x in zip(parts, xs):
            hq = _adaln_pre(x, ngains_ref[0:1, :], nshift_ref[...], nscale_ref[...]).astype(BF16)
            q = jnp.dot(hq, wq_ref[...], preferred_element_type=F32) * (SWA_HEAD_DIM ** -0.5)
            q_ref[:, sl] = q.T.astype(BF16)

    if tail == "none":
        (xo_ref,) = rest
        store_x(xo_ref)
    elif tail == "kv_q":
        (kvgain_ref, kvshift_ref, kvscale_ref, wkv_ref, ngains_ref, nshift_ref, nscale_ref, wq_ref,
         xo_ref, k_ref, v_ref, q_ref) = rest
        store_x(xo_ref)
        for sl, x in zip(parts, xs):
            hk = _adaln_pre(x, kvgain_ref[...], kvshift_ref[...], kvscale_ref[...]).astype(BF16)
            kv = jnp.dot(hk, wkv_ref[...], preferred_element_type=F32)
            kvw = kv.shape[1] // 2
            k_ref[sl, :] = kv[:, :kvw].astype(BF16)
            v_ref[:, sl] = kv[:, kvw:].T.astype(BF16)
        q_proj(ngains_ref, nshift_ref, nscale_ref, wq_ref, q_ref)
    elif tail == "q":
        (ngains_ref, nshift_ref, nscale_ref, wq_ref, xo_ref, q_ref) = rest
        store_x(xo_ref)
        q_proj(ngains_ref, nshift_ref, nscale_ref, wq_ref, q_ref)
    else:
        raise ValueError(tail)


def post_mlp(x, act, mods, layer, gains, w_o, wo_idx, w1, w2, tm, *, gate_args=None,
             tail="none", tail_args=None, n_split=1):
    B, S, D = x.shape
    gated = gate_args is not None
    row = lambda width: pl.BlockSpec((None, tm, width), lambda b, s: (b, s, 0))
    col = lambda width: pl.BlockSpec((None, width, tm), lambda b, s: (b, 0, s))
    ins, in_specs = [x], [row(D)]
    if gated:
        z_src, z_group, onorm, oidx = gate_args
        H = act.shape[1]
        ins += [act, z_src, onorm]
        in_specs += [pl.BlockSpec((None, H, tm, LANES), lambda b, s: (b, 0, s, 0)),
                     pl.BlockSpec((None, H, tm, LANES), lambda b, s: (b, z_group, s, 0)),
                     _layer_resident(onorm, oidx)]
    else:
        ins += [act]
        in_specs += [row(act.shape[-1])]
    ins += [w_o, w1, w2, gains, mods, mods, mods, mods]
    in_specs += [_layer_resident(w_o, wo_idx), _layer_resident(w1, layer), _layer_resident(w2, layer),
                 _layer_resident(gains, layer),
                 _mod_spec(D, layer, 2), _mod_spec(D, layer, 3), _mod_spec(D, layer, 4), _mod_spec(D, layer, 5)]
    out_specs = [row(D)]
    out_shape = [jax.ShapeDtypeStruct((B, S, D), F32)]
    weight_bytes = (w_o[0].size + w1[0].size + w2[0].size) * 2
    extra = 0
    if tail == "kv_q":
        kvmods, kvgain, w_kv, nlayer, w_q, qi = tail_args
        kv_spec = lambda k: pl.BlockSpec((None, None, 1, D), lambda b, s: (0, b, 0, k))
        ins += [kvgain.reshape(1, D), kvmods, kvmods, w_kv, gains, mods, mods, w_q]
        in_specs += [_resident((1, D)), kv_spec(0), kv_spec(1), _resident(w_kv.shape),
                     _layer_resident(gains, nlayer), _mod_spec(D, nlayer, 0), _mod_spec(D, nlayer, 1),
                     _layer_resident(w_q, qi)]
        kvw = w_kv.shape[1] // 2
        qw = w_q.shape[-1]
        out_specs += [row(kvw), col(kvw), col(qw)]
        out_shape += [jax.ShapeDtypeStruct((B, S, kvw), BF16), jax.ShapeDtypeStruct((B, kvw, S), BF16),
                      jax.ShapeDtypeStruct((B, qw, S), BF16)]
        weight_bytes += (w_kv.size + w_q[0].size) * 2
        extra = 2 * tm * (2 * kvw + qw) * 2
    elif tail == "q":
        nlayer, w_q, qi = tail_args
        qw = w_q.shape[-1]
        ins += [gains, mods, mods, w_q]
        in_specs += [_layer_resident(gains, nlayer), _mod_spec(D, nlayer, 0), _mod_spec(D, nlayer, 1),
                     _layer_resident(w_q, qi)]
        out_specs += [col(qw)]
        out_shape += [jax.ShapeDtypeStruct((B, qw, S), BF16)]
        weight_bytes += w_q[0].size * 2
        extra = 2 * tm * qw * 2
    ff_chunk = 512
    vmem = (weight_bytes + extra + 4 * tm * D * 4 + 4 * tm * D * 2 + 6 * tm * D * 4
            + 3 * tm * ff_chunk * 4 + (4 << 20))
    return pl.pallas_call(
        functools.partial(_post_mlp_kernel, gated=gated, tail=tail, ff_chunk=ff_chunk, n_split=n_split),
        grid=(B, S // tm),
        in_specs=in_specs, out_specs=out_specs, out_shape=out_shape,
        compiler_params=_cparams(("arbitrary", "arbitrary"), vmem),
        name="post_mlp_" + tail,
    )(*ins)


def kernel(x, c, mod_w, mod_b, norm_g, gdn_w_in, gdn_conv, gdn_a_log, gdn_dt_bias, gdn_onorm,
           gdn_w_out, kv_mod_w, kv_mod_b, kv_norm, w_kv, swa_w_q, swa_sinks, swa_w_o,
           mlp_w1, mlp_w2):
    B, S, D = x.shape
    depth = mod_w.shape[0]
    n_gdn = gdn_w_in.shape[0]
    tm = min(S, 512)
    tm_wide = min(S, 1024)
    rows = min(S, 256)
    assert S % tm == 0 and S % rows == 0 and rows % GDN_CHUNK == 0 and S % SWA_WINDOW == 0

    mods = modulation(c, mod_w, mod_b).reshape(depth, B, 1, mod_w.shape[-1])
    kvmods = modulation(c, kv_mod_w[None], kv_mod_b[None]).reshape(1, B, 1, kv_mod_w.shape[-1])

    w1 = mlp_w1.astype(BF16)
    w2 = mlp_w2.astype(BF16)
    gp = _gdn_params(gdn_w_in, gdn_conv, gdn_a_log, gdn_dt_bias)
    gdn_wo = gdn_w_out.astype(BF16)
    swa_wq = swa_w_q.astype(BF16)
    swa_wo = swa_w_o.astype(BF16)
    wkv = w_kv.astype(BF16)
    sinks = swa_sinks.astype(F32)
    gains = norm_g.astype(F32)
    onorm = gdn_onorm.astype(F32).reshape(n_gdn, 1, LANES)

    p, gb, gbt = gdn_in_proj(x, mods, 0, gains, gp, 0, tm)
    z_group = 3
    k_sh = v_sh = q = None
    for layer in range(depth):
        nxt = layer + 1
        if layer < n_gdn:
            act = gdn_core(p, gb, gbt, rows)
            w_o, wo_idx, gate_args = gdn_wo, layer, (p, z_group, onorm, layer)
        else:
            act = swa_attention(q, k_sh, v_sh, sinks, layer - n_gdn)
            w_o, wo_idx, gate_args = swa_wo, layer - n_gdn, None
        if nxt == depth:
            tail, tail_args = "none", None
        elif nxt < n_gdn:
            tail, tail_args = "none", None
        elif nxt == n_gdn:
            tail, tail_args = "kv_q", (kvmods, kv_norm, wkv, nxt, swa_wq, 0)
        else:
            tail, tail_args = "q", (nxt, swa_wq, nxt - n_gdn)
        wide = gate_args is None or tail == "none"
        outs = post_mlp(x, act, mods, layer, gains, w_o, wo_idx, w1, w2, tm_wide if wide else tm,
                        gate_args=gate_args, tail=tail, tail_args=tail_args,
                        n_split=tm_wide // tm if wide else 1)
        x = outs[0]
        if nxt < n_gdn:
            p, gb, gbt = gdn_in_proj(x, mods, nxt, gains, gp, nxt, tm)
        elif tail == "kv_q":
            k_sh, v_sh, q = outs[1:]
        elif tail == "q":
            (q,) = outs[1:]
    return x
```

```python
import functools

import jax
import jax.numpy as jnp
from jax import lax
from jax.experimental import pallas as pl
from jax.experimental.pallas import tpu as pltpu

F32 = jnp.float32
BF16 = jnp.bfloat16

NORM_EPS = 1e-6
LOG2_E = 1.4426950408889634
LANES = 128
SUBLANES = 8
GDN_HEADS = 8
GDN_DK = 128
GDN_CONV = 4
SWA_HEAD_DIM = 64
SWA_Q_HEADS = 16
SWA_KV_HEADS = 4
SWA_WINDOW = 128
SWA_BLOCKS_PER_STEP = 8
V7X_SCOPED_VMEM_BYTES = 60000 * 1024

GDN_CHUNK = 128
GDN_SLAB_GROUP = 2
CONV_HALO = SUBLANES


def _cparams(semantics, vmem_bytes):
    return pltpu.CompilerParams(dimension_semantics=semantics,
                                vmem_limit_bytes=min(int(vmem_bytes), V7X_SCOPED_VMEM_BYTES))


def _sigmoid(x):
    return 1.0 / (1.0 + jnp.exp2(x * (-LOG2_E)))


def _silu(x):
    return x * _sigmoid(x)


def _rms(x, eps=NORM_EPS):
    return x * lax.rsqrt(jnp.mean(x * x, axis=-1, keepdims=True) + eps)


def _bdot(a, b):
    return jnp.dot(a.astype(BF16), b.astype(BF16), preferred_element_type=F32)


def _bmm(a, b):
    return jnp.einsum("bij,bjk->bik", a.astype(BF16), b.astype(BF16), preferred_element_type=F32)


def _resident(shape):
    nd = len(shape)
    return pl.BlockSpec(shape, lambda *_: (0,) * nd, pipeline_mode=pl.Buffered(1))


def _layer_resident(arr, layer, block=None):
    block = tuple(arr.shape[1:]) if block is None else tuple(block)
    nd = len(block)
    return pl.BlockSpec((None,) + block, lambda *_: (layer,) + (0,) * nd, pipeline_mode=pl.Buffered(1))


def _mod_kernel(c_ref, w_ref, b_ref, o_ref):
    o_ref[0] = _bdot(_silu(c_ref[...]), w_ref[0]) + b_ref[0]


def modulation(c, w, b):
    L, D, N = w.shape
    B = c.shape[0]
    tn = min(N, 1024)
    assert N % tn == 0
    return pl.pallas_call(
        _mod_kernel,
        grid=(L, N // tn),
        in_specs=[pl.BlockSpec((B, D), lambda l, n: (0, 0)),
                  pl.BlockSpec((1, D, tn), lambda l, n: (l, 0, n)),
                  pl.BlockSpec((1, 1, tn), lambda l, n: (l, 0, n))],
        out_specs=pl.BlockSpec((1, B, tn), lambda l, n: (l, 0, n)),
        out_shape=jax.ShapeDtypeStruct((L, B, N), F32),
        compiler_params=_cparams(("arbitrary", "arbitrary"), 4 * D * tn * 4 + (4 << 20)),
        name="modulation",
    )(c, w, b.reshape(L, 1, N))


def _mod_spec(D, layer, k):
    return pl.BlockSpec((None, None, 1, D), lambda b, *_: (layer, b, 0, k))


def _adaln_pre(x, gain, shift, scale):
    return _rms(x) * (gain * (1.0 + scale)) + shift


def _adaln_post(x, y, gain, gate):
    return x + _rms(y) * ((1.0 + gate) * gain)


def _gdn_gates(ab, alog_ref, dtb_ref, out_gb_ref, out_gbt_ref, chunk):
    tm = ab.shape[0]
    lane = lax.broadcasted_iota(jnp.int32, ab.shape, 1)
    pre = ab + dtb_ref[...]
    softplus = jnp.maximum(pre, 0.0) + jnp.log(1.0 + jnp.exp(-jnp.abs(pre)))
    g = -jnp.exp(alog_ref[...]) * softplus
    g = jnp.where(lane < GDN_HEADS, g, 0.0)
    beta = _sigmoid(ab)
    r = lax.broadcasted_iota(jnp.int32, (chunk, chunk), 0)
    c = lax.broadcasted_iota(jnp.int32, (chunk, chunk), 1)
    tri = (r >= c).astype(F32)
    parts = []
    for i in range(tm // chunk):
        parts.append(jnp.dot(tri, g[i * chunk:(i + 1) * chunk], preferred_element_type=F32,
                             precision=lax.Precision.HIGHEST))
    gcum = jnp.concatenate(parts, axis=0) if len(parts) > 1 else parts[0]
    gb = jnp.where(lane < GDN_HEADS, gcum, jnp.where(lane < 2 * GDN_HEADS, beta, 0.0))
    out_gb_ref[...] = gb
    out_gbt_ref[...] = gb.T[:2 * GDN_HEADS]


def _conv_epilogue(g0, slabs, convw_ref, stage_ref, buf, carry_ref, out_ref):
    group = len(slabs)
    tm = slabs[0].shape[0]
    stage_ref[buf, :, 0:CONV_HALO, :] = carry_ref[g0:g0 + group]
    for j in range(group):
        stage_ref[buf, j, CONV_HALO:CONV_HALO + tm, :] = slabs[j]
    half = None
    for t in range(GDN_CONV):
        off = CONV_HALO - (GDN_CONV - 1) + t
        term = (0.5 * convw_ref[t, g0:g0 + group]) * stage_ref[buf, :, off:off + tm, :]
        half = term if half is None else half + term
    carry_ref[g0:g0 + group] = stage_ref[buf, :, tm:tm + CONV_HALO, :]
    y = half + half * jnp.tanh(half)
    if g0 < 2 * GDN_HEADS:
        y = y * lax.rsqrt(jnp.sum(y * y, axis=-1, keepdims=True) + NORM_EPS)
    out_ref[g0:g0 + group] = y.astype(BF16)


def _gdn_group_order(n_slabs):
    group = GDN_SLAB_GROUP
    n_conv = 3 * GDN_HEADS // group
    conv_groups, z_groups = list(range(n_conv)), list(range(n_conv, n_slabs // group))
    per_z = -(-n_conv // max(len(z_groups), 1))
    order = []
    while conv_groups or z_groups:
        order += conv_groups[:per_z]
        conv_groups = conv_groups[per_z:]
        if z_groups:
            order.append(z_groups.pop(0))
    return order


def _gdn_in_body(x, first_row_block, gain, shift_ref, scale_ref, w_ref, wab_ref, alog_ref, dtb_ref,
                 convw_ref, p_ref, gb_ref, gbt_ref, stage_ref, carry_ref, chunk):
    group = GDN_SLAB_GROUP

    @pl.when(first_row_block)
    def _():
        carry_ref[...] = jnp.zeros_like(carry_ref)

    h = _adaln_pre(x, gain, shift_ref[...], scale_ref[...]).astype(BF16)

    def epilogue(gi, res):
        g0 = gi * group
        slabs = [res[:, j * LANES:(j + 1) * LANES] for j in range(group)]
        if g0 >= 3 * GDN_HEADS:
            for j in range(group):
                p_ref[g0 + j] = slabs[j].astype(BF16)
        else:
            _conv_epilogue(g0, slabs, convw_ref, stage_ref, gi % 2, carry_ref, p_ref)

    pending = None
    for gi in _gdn_group_order(w_ref.shape[1] // LANES):
        g0 = gi * group
        res = jnp.dot(h, w_ref[:, g0 * LANES:(g0 + group) * LANES], preferred_element_type=F32)
        if pending is not None:
            epilogue(*pending)
        pending = (gi, res)
    ab = jnp.dot(h, wab_ref[...], preferred_element_type=F32)
    epilogue(*pending)
    _gdn_gates(ab, alog_ref, dtb_ref, gb_ref, gbt_ref, chunk)


def _gdn_in_kernel(x_ref, gains_ref, shift_ref, scale_ref, w_ref, wab_ref, alog_ref, dtb_ref, convw_ref,
                   p_ref, gb_ref, gbt_ref, stage_ref, carry_ref, *, chunk):
    _gdn_in_body(x_ref[...], pl.program_id(1) == 0, gains_ref[0:1, :], shift_ref, scale_ref, w_ref, wab_ref,
                 alog_ref, dtb_ref, convw_ref, p_ref, gb_ref, gbt_ref, stage_ref, carry_ref, chunk)


def _gdn_in_outputs(B, S, tm, n_slabs):
    out_specs = [pl.BlockSpec((None, n_slabs, tm, LANES), lambda b, s: (b, 0, s, 0)),
                 pl.BlockSpec((None, tm, LANES), lambda b, s: (b, s, 0)),
                 pl.BlockSpec((None, 2 * GDN_HEADS, tm), lambda b, s: (b, 0, s))]
    out_shape = [jax.ShapeDtypeStruct((B, n_slabs, S, LANES), BF16),
                 jax.ShapeDtypeStruct((B, S, LANES), F32),
                 jax.ShapeDtypeStruct((B, 2 * GDN_HEADS, S), F32)]
    return out_specs, out_shape


def _gdn_in_scratch(tm):
    return [pltpu.VMEM((2, GDN_SLAB_GROUP, tm + CONV_HALO, LANES), F32),
            pltpu.VMEM((3 * GDN_HEADS, CONV_HALO, LANES), F32)]


def _gdn_in_scratch_bytes(tm):
    return 2 * GDN_SLAB_GROUP * (tm + CONV_HALO) * LANES * 4 + 3 * GDN_HEADS * CONV_HALO * LANES * 4


def _gdn_params(w_in, conv_w, a_log, dt_bias):
    n, D, n_in = w_in.shape
    main = n_in - 2 * GDN_HEADS
    w_all = w_in.astype(BF16)
    w_ab = jnp.zeros((n, D, LANES), F32).at[:, :, :2 * GDN_HEADS].set(w_in[:, :, main:]).astype(BF16)
    alog = jnp.zeros((n, 1, LANES), F32).at[:, 0, :GDN_HEADS].set(a_log.astype(F32))
    dtb = jnp.zeros((n, 1, LANES), F32).at[:, 0, :GDN_HEADS].set(dt_bias.astype(F32))
    convw = conv_w.astype(F32).reshape(n, GDN_CONV, main // LANES - GDN_HEADS, 1, LANES)
    return dict(w_all=w_all, main=main, w_ab=w_ab, alog=alog, dtb=dtb, convw=convw)


def _gdn_in_inputs(gp, i):
    D = gp["w_all"].shape[1]
    ins = [gp["w_all"], gp["w_ab"], gp["alog"], gp["dtb"], gp["convw"]]
    specs = [_layer_resident(gp["w_all"], i, (D, gp["main"])), _layer_resident(gp["w_ab"], i),
             _layer_resident(gp["alog"], i), _layer_resident(gp["dtb"], i), _layer_resident(gp["convw"], i)]
    weight_bytes = (D * gp["main"] + D * LANES) * 2 + gp["convw"][0].size * 4 * SUBLANES
    return ins, specs, weight_bytes


def gdn_in_proj(x, mods, layer, gains, gp, i, tm):
    B, S, D = x.shape
    n_slabs = gp["main"] // LANES
    out_specs, out_shape = _gdn_in_outputs(B, S, tm, n_slabs)
    w_ins, w_specs, weight_bytes = _gdn_in_inputs(gp, i)
    vmem = (2 * tm * D * 4 + weight_bytes + 2 * tm * n_slabs * LANES * 2 + _gdn_in_scratch_bytes(tm)
            + 6 * tm * GDN_SLAB_GROUP * LANES * 4 + (6 << 20))
    return pl.pallas_call(
        functools.partial(_gdn_in_kernel, chunk=GDN_CHUNK),
        grid=(B, S // tm),
        in_specs=[pl.BlockSpec((None, tm, D), lambda b, s: (b, s, 0)),
                  _layer_resident(gains, layer),
                  _mod_spec(D, layer, 0), _mod_spec(D, layer, 1)] + w_specs,
        out_specs=out_specs, out_shape=out_shape,
        scratch_shapes=_gdn_in_scratch(tm),
        compiler_params=_cparams(("arbitrary", "arbitrary"), vmem),
        name="gdn_in_proj",
    )(x, gains, mods, mods, *w_ins)


def _unit_lower_inverse(lmat, ri, ci):
    n = lmat.shape[-1]
    diff = ri ^ ci
    p = jnp.where(ri == ci, 1.0, 0.0) - jnp.where(diff < 2, lmat, 0.0)
    s = 2
    while s < n:
        off = jnp.where(diff >= s, jnp.where(diff < 2 * s, lmat, 0.0), 0.0)
        p = p - _bmm(p, _bmm(off, p))
        s *= 2
    return p


def _gdn_core_kernel(q_ref, k_ref, v_ref, gb_ref, gbt_ref, o_ref, state_ref, *, rows, chunk):
    H = GDN_HEADS
    nc = rows // chunk

    @pl.when(pl.program_id(1) == 0)
    def _():
        state_ref[...] = jnp.zeros_like(state_ref)

    def chunked(t):
        if nc == 1:
            return t
        return jnp.concatenate([t[:, c * chunk:(c + 1) * chunk] for c in range(nc)], axis=0)

    q = chunked(q_ref[...])
    k = chunked(k_ref[...])
    v = chunked(v_ref[...])

    gb = gb_ref[...]
    gbt = gbt_ref[...]
    gcc = chunked(jnp.stack([gb[:, h:h + 1] for h in range(H)]))
    bc = chunked(jnp.stack([gb[:, H + h:H + h + 1] for h in range(H)]))

    def rows_of(base):
        return jnp.stack([gbt[base + h:base + h + 1, c * chunk:(c + 1) * chunk]
                          for c in range(nc) for h in range(H)])

    grc, brc = rows_of(0), rows_of(H)
    g_last = gcc[:, chunk - 1:chunk, :]

    ri = lax.broadcasted_iota(jnp.int32, (1, chunk, chunk), 1)
    ci = lax.broadcasted_iota(jnp.int32, (1, chunk, chunk), 2)
    gcb = jnp.broadcast_to(gcc, (nc * H, chunk, LANES))
    decay = jnp.exp(jnp.where(ri >= ci, gcb[:, :, :chunk] - grc, -jnp.inf))
    kk_qk = jnp.einsum("bcd,bsd->bcs", jnp.concatenate([k, q], axis=1), k,
                       preferred_element_type=F32)
    lmat = jnp.where(ri > ci, kk_qk[:, :chunk] * decay * bc, 0.0)
    attn = (kk_qk[:, chunk:] * decay).astype(BF16)
    tinv = _unit_lower_inverse(lmat, ri, ci)
    u = _bmm(tinv * brc, v)
    w = _bmm(tinv * (brc * jnp.exp(grc)), k)
    wq = jnp.concatenate([w.astype(BF16), q], axis=1)
    eg = jnp.exp(gcb)
    kt = jnp.stack([k[b].astype(F32).T for b in range(nc * H)])
    kgt = (kt * jnp.exp(g_last - grc)).astype(BF16)
    egl = jnp.exp(g_last)

    state = state_ref[...]
    for c in range(nc):
        sel = slice(c * H, (c + 1) * H)
        ws_qs = _bmm(wq[sel], state)
        v_new = (u[sel] - ws_qs[:, :chunk]).astype(BF16)
        o = ws_qs[:, chunk:] * eg[sel] + _bmm(attn[sel], v_new)
        state = state * egl[sel] + _bmm(kgt[sel], v_new)
        o_ref[:, c * chunk:(c + 1) * chunk, :] = o.astype(BF16)
    state_ref[...] = state


def gdn_core(p, gb, gbt, rows):
    B, _, S, _ = p.shape
    H = GDN_HEADS

    def slabs(group):
        return pl.BlockSpec((None, H, rows, LANES), lambda b, s: (b, group, s, 0))

    return pl.pallas_call(
        functools.partial(_gdn_core_kernel, rows=rows, chunk=GDN_CHUNK),
        grid=(B, S // rows),
        in_specs=[slabs(0), slabs(1), slabs(2),
                  pl.BlockSpec((None, rows, LANES), lambda b, s: (b, s, 0)),
                  pl.BlockSpec((None, 2 * H, rows), lambda b, s: (b, 0, s))],
        out_specs=pl.BlockSpec((None, H, rows, LANES), lambda b, s: (b, 0, s, 0)),
        out_shape=jax.ShapeDtypeStruct((B, H, S, LANES), BF16),
        scratch_shapes=[pltpu.VMEM((H, GDN_DK, LANES), F32)],
        compiler_params=_cparams(("arbitrary", "arbitrary"), 32 << 20),
        name="gdn_core",
    )(p, p, p, gb, gbt)


def _alibi_slope(head):
    return 2.0 ** (-8.0 * (head + 1) / SWA_Q_HEADS)


def _swa_kernel(sink_ref, qt_ref, kp_ref, kc_ref, vtp_ref, vtc_ref, o_ref, bias_ref, *, layer_slot, n_blocks):
    blk = pl.program_id(1)
    W = SWA_WINDOW
    hd = SWA_HEAD_DIM
    KV = SWA_KV_HEADS
    G = SWA_Q_HEADS // KV
    NEG = -jnp.inf

    @pl.when(jnp.logical_and(pl.program_id(0) == 0, blk == 0))
    def _():
        key = lax.broadcasted_iota(jnp.int32, (2 * W, G * W), 0)
        col = lax.broadcasted_iota(jnp.int32, (2 * W, G * W), 1)
        dist = (col & (W - 1)) + W - key
        valid = (dist >= 0) & (dist < SWA_WINDOW)
        distf = dist.astype(F32)
        for j in range(KV):
            slope = jnp.zeros((2 * W, G * W), F32)
            for a in range(G):
                slope = jnp.where((col >= a * W) & (col < (a + 1) * W), _alibi_slope(G * j + a), slope)
            bias = jnp.where(valid, (-LOG2_E) * slope * distf, NEG)
            bias_ref[1, j] = bias
            bias_ref[0, j] = jnp.where(key >= W, bias, NEG)

    kall = jnp.concatenate([kp_ref[...], kc_ref[...]], axis=0)
    vtall = jnp.concatenate([vtp_ref[...], vtc_ref[...]], axis=1)
    zpad = jnp.zeros((hd, W), BF16)
    ks, ws, vts = [], [], []
    for i in range(n_blocks):
        for j in range(KV):
            tile, half = divmod(j, LANES // hd)
            ks.append(kall[i * W:(i + 2) * W, tile * LANES:(tile + 1) * LANES])
            cols = []
            for a in range(G):
                h = G * j + a
                qt = qt_ref[h * hd:(h + 1) * hd, i * W:(i + 1) * W]
                cols.append(jnp.concatenate([qt, zpad] if half == 0 else [zpad, qt], axis=0))
            ws.append(jnp.concatenate(cols, axis=1))
            vts.append(vtall[j * hd:(j + 1) * hd, i * W:(i + 2) * W])
    ks, ws, vts = jnp.stack(ks), jnp.stack(ws), jnp.stack(vts)

    s = jnp.einsum("jkd,jdq->jkq", ks, ws, preferred_element_type=F32)
    bias = [bias_ref[jnp.minimum(blk, 1)]] + [bias_ref[1]] * (n_blocks - 1)
    s = s + (jnp.concatenate(bias, axis=0) if n_blocks > 1 else bias[0])
    lane = lax.broadcasted_iota(jnp.int32, (1, G * W), 1)
    sinks = []
    for j in range(KV):
        row = jnp.zeros((1, G * W), F32)
        for a in range(G):
            row = jnp.where((lane >= a * W) & (lane < (a + 1) * W), sink_ref[layer_slot, G * j + a] * LOG2_E, row)
        sinks.append(row)
    sink = jnp.stack(sinks * n_blocks)
    m = jnp.maximum(jnp.max(s, axis=1, keepdims=True), sink)
    p = jnp.exp2(s - m)
    denom = jnp.sum(p, axis=1, keepdims=True) + jnp.exp2(sink - m)
    ot = jnp.einsum("jdk,jkq->jdq", vts, p.astype(BF16), preferred_element_type=F32)
    ot = ot * (1.0 / denom)
    for i in range(n_blocks):
        for t in range(SWA_Q_HEADS // 2):
            j, a0 = divmod(2 * t, G)
            oj = ot[i * KV + j]
            pair = jnp.concatenate([oj[:, a0 * W:(a0 + 1) * W], oj[:, (a0 + 1) * W:(a0 + 2) * W]], axis=0)
            o_ref[i * W:(i + 1) * W, t * LANES:(t + 1) * LANES] = pair.T.astype(BF16)


def swa_attention(qt, k_sh, vt_sh, sinks, layer_slot):
    B, QW, S = qt.shape
    KW = k_sh.shape[-1]
    W = SWA_WINDOW
    G = SWA_Q_HEADS // SWA_KV_HEADS
    nb = SWA_BLOCKS_PER_STEP if S % (SWA_BLOCKS_PER_STEP * W) == 0 else 1
    T = nb * W
    prev = lambda n: jnp.maximum(n * nb - 1, 0)
    return pl.pallas_call(
        functools.partial(_swa_kernel, layer_slot=layer_slot, n_blocks=nb),
        grid=(B, S // T),
        in_specs=[pl.BlockSpec(memory_space=pltpu.SMEM),
                  pl.BlockSpec((None, QW, T), lambda b, n: (b, 0, n)),
                  pl.BlockSpec((None, W, KW), lambda b, n: (b, prev(n), 0)),
                  pl.BlockSpec((None, T, KW), lambda b, n: (b, n, 0)),
                  pl.BlockSpec((None, KW, W), lambda b, n: (b, 0, prev(n))),
                  pl.BlockSpec((None, KW, T), lambda b, n: (b, 0, n))],
        out_specs=pl.BlockSpec((None, T, QW), lambda b, n: (b, n, 0)),
        out_shape=jax.ShapeDtypeStruct((B, S, QW), BF16),
        scratch_shapes=[pltpu.VMEM((2, SWA_KV_HEADS, 2 * W, G * W), F32)],
        compiler_params=_cparams(("arbitrary", "arbitrary"), (8 + 6 * nb) << 20),
        name="swa_attention",
    )(sinks, qt, k_sh, k_sh, vt_sh, vt_sh)


def _post_mlp_kernel(*refs, gated, tail, ff_chunk, n_split):
    n_fixed = 12 if gated else 10
    if gated:
        (x_ref, a_ref, z_ref, onorm_ref, wo_ref, w1_ref, w2_ref, gains_ref,
         gate_mix_ref, shift_mlp_ref, scale_mlp_ref, gate_mlp_ref) = refs[:n_fixed]
    else:
        (x_ref, a_ref, wo_ref, w1_ref, w2_ref, gains_ref,
         gate_mix_ref, shift_mlp_ref, scale_mlp_ref, gate_mlp_ref) = refs[:n_fixed]
    rest = refs[n_fixed:]
    sub = x_ref.shape[0] // n_split
    parts = [slice(i * sub, (i + 1) * sub) for i in range(n_split)]
    xs = [x_ref[sl, :] for sl in parts]
    if gated:
        acts = []
        for sl in parts:
            g = (_rms(a_ref[:, sl, :].astype(F32), NORM_EPS * GDN_DK) * onorm_ref[...]
                 * _silu(z_ref[:, sl, :].astype(F32))).astype(BF16)
            acts.append(jnp.concatenate([g[h] for h in range(g.shape[0])], axis=1))
    else:
        acts = [a_ref[sl, :] for sl in parts]
    ys = [jnp.dot(act, wo_ref[...], preferred_element_type=F32) for act in acts]
    xs = [_adaln_post(x, y, gains_ref[1:2, :], gate_mix_ref[...]) for x, y in zip(xs, ys)]
    hs = [_adaln_pre(x, gains_ref[2:3, :], shift_mlp_ref[...], scale_mlp_ref[...]).astype(BF16) for x in xs]
    d_ff = w1_ref.shape[1]
    accs = [None] * n_split
    for f0 in range(0, d_ff, ff_chunk):
        for i in range(n_split):
            a = jnp.dot(hs[i], w1_ref[:, f0:f0 + ff_chunk], preferred_element_type=F32)
            a = jnp.square(jnp.maximum(a, 0.0)).astype(BF16)
            part = jnp.dot(a, w2_ref[f0:f0 + ff_chunk, :], preferred_element_type=F32)
            accs[i] = part if accs[i] is None else accs[i] + part
    xs = [_adaln_post(x, acc, gains_ref[3:4, :], gate_mlp_ref[...]) for x, acc in zip(xs, accs)]

    def store_x(xo_ref):
        for sl, x in zip(parts, xs):
            xo_ref[sl, :] = x

    def q_proj(ngains_ref, nshift_ref, nscale_ref, wq_ref, q_ref):
        for sl, x in zip(parts, xs):
            hq = _adaln_pre(x, ngains_ref[0:1, :], nshift_ref[...], nscale_ref[...]).astype(BF16)
            q = jnp.dot(hq, wq_ref[...], preferred_element_type=F32) * (SWA_HEAD_DIM ** -0.5 * LOG2_E)
            q_ref[:, sl] = q.T.astype(BF16)

    if tail == "none":
        (xo_ref,) = rest
        store_x(xo_ref)
    elif tail == "kv_q":
        (kvgain_ref, kvshift_ref, kvscale_ref, wkv_ref, ngains_ref, nshift_ref, nscale_ref, wq_ref,
         xo_ref, k_ref, v_ref, q_ref) = rest
        store_x(xo_ref)
        for sl, x in zip(parts, xs):
            hk = _adaln_pre(x, kvgain_ref[...], kvshift_ref[...], kvscale_ref[...]).astype(BF16)
            kv = jnp.dot(hk, wkv_ref[...], preferred_element_type=F32)
            kvw = kv.shape[1] // 2
            k_ref[sl, :] = kv[:, :kvw].astype(BF16)
            v_ref[:, sl] = kv[:, kvw:].T.astype(BF16)
        q_proj(ngains_ref, nshift_ref, nscale_ref, wq_ref, q_ref)
    elif tail == "q":
        (ngains_ref, nshift_ref, nscale_ref, wq_ref, xo_ref, q_ref) = rest
        store_x(xo_ref)
        q_proj(ngains_ref, nshift_ref, nscale_ref, wq_ref, q_ref)
    else:
        raise ValueError(tail)


def post_mlp(x, act, mods, layer, gains, w_o, wo_idx, w1, w2, tm, *, gate_args=None,
             tail="none", tail_args=None, n_split=1):
    B, S, D = x.shape
    gated = gate_args is not None
    row = lambda width: pl.BlockSpec((None, tm, width), lambda b, s: (b, s, 0))
    col = lambda width: pl.BlockSpec((None, width, tm), lambda b, s: (b, 0, s))
    ins, in_specs = [x], [row(D)]
    if gated:
        z_src, z_group, onorm, oidx = gate_args
        H = act.shape[1]
        ins += [act, z_src, onorm]
        in_specs += [pl.BlockSpec((None, H, tm, LANES), lambda b, s: (b, 0, s, 0)),
                     pl.BlockSpec((None, H, tm, LANES), lambda b, s: (b, z_group, s, 0)),
                     _layer_resident(onorm, oidx)]
    else:
        ins += [act]
        in_specs += [row(act.shape[-1])]
    ins += [w_o, w1, w2, gains, mods, mods, mods, mods]
    in_specs += [_layer_resident(w_o, wo_idx), _layer_resident(w1, layer), _layer_resident(w2, layer),
                 _layer_resident(gains, layer),
                 _mod_spec(D, layer, 2), _mod_spec(D, layer, 3), _mod_spec(D, layer, 4), _mod_spec(D, layer, 5)]
    out_specs = [row(D)]
    out_shape = [jax.ShapeDtypeStruct((B, S, D), F32)]
    weight_bytes = (w_o[0].size + w1[0].size + w2[0].size) * 2
    extra = 0
    if tail == "kv_q":
        kvmods, kvgain, w_kv, nlayer, w_q, qi = tail_args
        kv_spec = lambda k: pl.BlockSpec((None, None, 1, D), lambda b, s: (0, b, 0, k))
        ins += [kvgain.reshape(1, D), kvmods, kvmods, w_kv, gains, mods, mods, w_q]
        in_specs += [_resident((1, D)), kv_spec(0), kv_spec(1), _resident(w_kv.shape),
                     _layer_resident(gains, nlayer), _mod_spec(D, nlayer, 0), _mod_spec(D, nlayer, 1),
                     _layer_resident(w_q, qi)]
        kvw = w_kv.shape[1] // 2
        qw = w_q.shape[-1]
        out_specs += [row(kvw), col(kvw), col(qw)]
        out_shape += [jax.ShapeDtypeStruct((B, S, kvw), BF16), jax.ShapeDtypeStruct((B, kvw, S), BF16),
                      jax.ShapeDtypeStruct((B, qw, S), BF16)]
        weight_bytes += (w_kv.size + w_q[0].size) * 2
        extra = 2 * tm * (2 * kvw + qw) * 2
    elif tail == "q":
        nlayer, w_q, qi = tail_args
        qw = w_q.shape[-1]
        ins += [gains, mods, mods, w_q]
        in_specs += [_layer_resident(gains, nlayer), _mod_spec(D, nlayer, 0), _mod_spec(D, nlayer, 1),
                     _layer_resident(w_q, qi)]
        out_specs += [col(qw)]
        out_shape += [jax.ShapeDtypeStruct((B, qw, S), BF16)]
        weight_bytes += w_q[0].size * 2
        extra = 2 * tm * qw * 2
    ff_chunk = 512
    vmem = (weight_bytes + extra + 4 * tm * D * 4 + 4 * tm * D * 2 + 6 * tm * D * 4
            + 3 * tm * ff_chunk * 4 + (4 << 20))
    return pl.pallas_call(
        functools.partial(_post_mlp_kernel, gated=gated, tail=tail, ff_chunk=ff_chunk, n_split=n_split),
        grid=(B, S // tm),
        in_specs=in_specs, out_specs=out_specs, out_shape=out_shape,
        compiler_params=_cparams(("arbitrary", "arbitrary"), vmem),
        name="post_mlp_" + tail,
    )(*ins)


def kernel(x, c, mod_w, mod_b, norm_g, gdn_w_in, gdn_conv, gdn_a_log, gdn_dt_bias, gdn_onorm,
           gdn_w_out, kv_mod_w, kv_mod_b, kv_norm, w_kv, swa_w_q, swa_sinks, swa_w_o,
           mlp_w1, mlp_w2):
    B, S, D = x.shape
    depth = mod_w.shape[0]
    n_gdn = gdn_w_in.shape[0]
    tm = min(S, 512)
    tm_wide = min(S, 1024)
    rows = min(S, 256)
    assert S % tm == 0 and S % rows == 0 and rows % GDN_CHUNK == 0 and S % SWA_WINDOW == 0

    mods = modulation(c, mod_w, mod_b).reshape(depth, B, 1, mod_w.shape[-1])
    kvmods = modulation(c, kv_mod_w[None], kv_mod_b[None]).reshape(1, B, 1, kv_mod_w.shape[-1])

    w1 = mlp_w1.astype(BF16)
    w2 = mlp_w2.astype(BF16)
    gp = _gdn_params(gdn_w_in, gdn_conv, gdn_a_log, gdn_dt_bias)
    gdn_wo = gdn_w_out.astype(BF16)
    swa_wq = swa_w_q.astype(BF16)
    swa_wo = swa_w_o.astype(BF16)
    wkv = w_kv.astype(BF16)
    sinks = swa_sinks.astype(F32)
    gains = norm_g.astype(F32)
    onorm = gdn_onorm.astype(F32).reshape(n_gdn, 1, LANES)

    p, gb, gbt = gdn_in_proj(x, mods, 0, gains, gp, 0, tm)
    z_group = 3
    k_sh = v_sh = q = None
    for layer in range(depth):
        nxt = layer + 1
        if layer < n_gdn:
            act = gdn_core(p, gb, gbt, rows)
            w_o, wo_idx, gate_args = gdn_wo, layer, (p, z_group, onorm, layer)
        else:
            act = swa_attention(q, k_sh, v_sh, sinks, layer - n_gdn)
            w_o, wo_idx, gate_args = swa_wo, layer - n_gdn, None
        if nxt == depth:
            tail, tail_args = "none", None
        elif nxt < n_gdn:
            tail, tail_args = "none", None
        elif nxt == n_gdn:
            tail, tail_args = "kv_q", (kvmods, kv_norm, wkv, nxt, swa_wq, 0)
        else:
            tail, tail_args = "q", (nxt, swa_wq, nxt - n_gdn)
        wide = gate_args is None or tail == "none"
        outs = post_mlp(x, act, mods, layer, gains, w_o, wo_idx, w1, w2, tm_wide if wide else tm,
                        gate_args=gate_args, tail=tail, tail_args=tail_args,
                        n_split=tm_wide // tm if wide else 1)
        x = outs[0]
        if nxt < n_gdn:
            p, gb, gbt = gdn_in_proj(x, mods, nxt, gains, gp, nxt, tm)
        elif tail == "kv_q":
            k_sh, v_sh, q = outs[1:]
        elif tail == "q":
            (q,) = outs[1:]
    return x
```

```python
import functools

import jax
import jax.numpy as jnp
from jax import lax
from jax.experimental import pallas as pl
from jax.experimental.pallas import tpu as pltpu

F32 = jnp.float32
BF16 = jnp.bfloat16

NORM_EPS = 1e-6
LOG2_E = 1.4426950408889634
LANES = 128
SUBLANES = 8
GDN_HEADS = 8
GDN_DK = 128
GDN_CONV = 4
SWA_HEAD_DIM = 64
SWA_Q_HEADS = 16
SWA_KV_HEADS = 4
SWA_WINDOW = 128
SWA_BLOCKS_PER_STEP = 8
V7X_SCOPED_VMEM_BYTES = 60000 * 1024

GDN_CHUNK = 128
GDN_SLAB_GROUP = 2
CONV_HALO = SUBLANES


def _cparams(semantics, vmem_bytes):
    return pltpu.CompilerParams(dimension_semantics=semantics,
                                vmem_limit_bytes=min(int(vmem_bytes), V7X_SCOPED_VMEM_BYTES))


def _sigmoid(x):
    return 1.0 / (1.0 + jnp.exp2(x * (-LOG2_E)))


def _silu(x):
    return x * _sigmoid(x)


def _rms(x, eps=NORM_EPS):
    return x * lax.rsqrt(jnp.mean(x * x, axis=-1, keepdims=True) + eps)


def _bdot(a, b):
    return jnp.dot(a.astype(BF16), b.astype(BF16), preferred_element_type=F32)


def _bmm(a, b):
    return jnp.einsum("bij,bjk->bik", a.astype(BF16), b.astype(BF16), preferred_element_type=F32)


def _resident(shape):
    nd = len(shape)
    return pl.BlockSpec(shape, lambda *_: (0,) * nd, pipeline_mode=pl.Buffered(1))


def _layer_resident(arr, layer, block=None):
    block = tuple(arr.shape[1:]) if block is None else tuple(block)
    nd = len(block)
    return pl.BlockSpec((None,) + block, lambda *_: (layer,) + (0,) * nd, pipeline_mode=pl.Buffered(1))


def _mod_kernel(c_ref, w_ref, b_ref, o_ref):
    o_ref[0] = _bdot(_silu(c_ref[...]), w_ref[0]) + b_ref[0]


def modulation(c, w, b):
    L, D, N = w.shape
    B = c.shape[0]
    tn = min(N, 1024)
    assert N % tn == 0
    return pl.pallas_call(
        _mod_kernel,
        grid=(L, N // tn),
        in_specs=[pl.BlockSpec((B, D), lambda l, n: (0, 0)),
                  pl.BlockSpec((1, D, tn), lambda l, n: (l, 0, n)),
                  pl.BlockSpec((1, 1, tn), lambda l, n: (l, 0, n))],
        out_specs=pl.BlockSpec((1, B, tn), lambda l, n: (l, 0, n)),
        out_shape=jax.ShapeDtypeStruct((L, B, N), F32),
        compiler_params=_cparams(("arbitrary", "arbitrary"), 4 * D * tn * 4 + (4 << 20)),
        name="modulation",
    )(c, w, b.reshape(L, 1, N))


def _mod_spec(D, layer, k):
    return pl.BlockSpec((None, None, 1, D), lambda b, *_: (layer, b, 0, k))


def _adaln_pre(x, gain, shift, scale):
    return _rms(x) * (gain * (1.0 + scale)) + shift


def _adaln_post(x, y, gain, gate):
    return x + _rms(y) * ((1.0 + gate) * gain)


def _gdn_gates(ab, alog_ref, dtb_ref, out_gb_ref, out_gbt_ref, chunk):
    tm = ab.shape[0]
    lane = lax.broadcasted_iota(jnp.int32, ab.shape, 1)
    pre = ab + dtb_ref[...]
    softplus = jnp.maximum(pre, 0.0) + jnp.log(1.0 + jnp.exp(-jnp.abs(pre)))
    g = -jnp.exp(alog_ref[...]) * softplus
    g = jnp.where(lane < GDN_HEADS, g, 0.0)
    beta = _sigmoid(ab)
    r = lax.broadcasted_iota(jnp.int32, (chunk, chunk), 0)
    c = lax.broadcasted_iota(jnp.int32, (chunk, chunk), 1)
    tri = (r >= c).astype(F32)
    parts = []
    for i in range(tm // chunk):
        parts.append(jnp.dot(tri, g[i * chunk:(i + 1) * chunk], preferred_element_type=F32,
                             precision=lax.Precision.HIGHEST))
    gcum = jnp.concatenate(parts, axis=0) if len(parts) > 1 else parts[0]
    gb = jnp.where(lane < GDN_HEADS, gcum, jnp.where(lane < 2 * GDN_HEADS, beta, 0.0))
    out_gb_ref[...] = gb
    out_gbt_ref[...] = gb.T[:2 * GDN_HEADS]


def _conv_epilogue(g0, slabs, convw_ref, stage_ref, buf, carry_ref, out_ref):
    group = len(slabs)
    tm = slabs[0].shape[0]
    stage_ref[buf, :, 0:CONV_HALO, :] = carry_ref[g0:g0 + group]
    for j in range(group):
        stage_ref[buf, j, CONV_HALO:CONV_HALO + tm, :] = slabs[j]
    half = None
    for t in range(GDN_CONV):
        off = CONV_HALO - (GDN_CONV - 1) + t
        term = (0.5 * convw_ref[t, g0:g0 + group]) * stage_ref[buf, :, off:off + tm, :]
        half = term if half is None else half + term
    carry_ref[g0:g0 + group] = stage_ref[buf, :, tm:tm + CONV_HALO, :]
    y = half + half * jnp.tanh(half)
    if g0 < 2 * GDN_HEADS:
        y = y * lax.rsqrt(jnp.sum(y * y, axis=-1, keepdims=True) + NORM_EPS)
    out_ref[g0:g0 + group] = y.astype(BF16)


def _gdn_group_order(n_slabs):
    group = GDN_SLAB_GROUP
    n_conv = 3 * GDN_HEADS // group
    conv_groups, z_groups = list(range(n_conv)), list(range(n_conv, n_slabs // group))
    per_z = -(-n_conv // max(len(z_groups), 1))
    order = []
    while conv_groups or z_groups:
        order += conv_groups[:per_z]
        conv_groups = conv_groups[per_z:]
        if z_groups:
            order.append(z_groups.pop(0))
    return order


def _gdn_in_body(x, first_row_block, gain, shift_ref, scale_ref, w_ref, wab_ref, alog_ref, dtb_ref,
                 convw_ref, p_ref, gb_ref, gbt_ref, stage_ref, carry_ref, chunk):
    group = GDN_SLAB_GROUP

    @pl.when(first_row_block)
    def _():
        carry_ref[...] = jnp.zeros_like(carry_ref)

    h = _adaln_pre(x, gain, shift_ref[...], scale_ref[...]).astype(BF16)

    def epilogue(gi, res):
        g0 = gi * group
        slabs = [res[:, j * LANES:(j + 1) * LANES] for j in range(group)]
        if g0 >= 3 * GDN_HEADS:
            for j in range(group):
                p_ref[g0 + j] = slabs[j].astype(BF16)
        else:
            _conv_epilogue(g0, slabs, convw_ref, stage_ref, gi % 2, carry_ref, p_ref)

    pending = None
    for gi in _gdn_group_order(w_ref.shape[1] // LANES):
        g0 = gi * group
        res = jnp.dot(h, w_ref[:, g0 * LANES:(g0 + group) * LANES], preferred_element_type=F32)
        if pending is not None:
            epilogue(*pending)
        pending = (gi, res)
    ab = jnp.dot(h, wab_ref[...], preferred_element_type=F32)
    epilogue(*pending)
    _gdn_gates(ab, alog_ref, dtb_ref, gb_ref, gbt_ref, chunk)


def _gdn_in_kernel(x_ref, gains_ref, shift_ref, scale_ref, w_ref, wab_ref, alog_ref, dtb_ref, convw_ref,
                   p_ref, gb_ref, gbt_ref, stage_ref, carry_ref, *, chunk):
    _gdn_in_body(x_ref[...], pl.program_id(1) == 0, gains_ref[0:1, :], shift_ref, scale_ref, w_ref, wab_ref,
                 alog_ref, dtb_ref, convw_ref, p_ref, gb_ref, gbt_ref, stage_ref, carry_ref, chunk)


def _gdn_in_outputs(B, S, tm, n_slabs):
    out_specs = [pl.BlockSpec((None, n_slabs, tm, LANES), lambda b, s: (b, 0, s, 0)),
                 pl.BlockSpec((None, tm, LANES), lambda b, s: (b, s, 0)),
                 pl.BlockSpec((None, 2 * GDN_HEADS, tm), lambda b, s: (b, 0, s))]
    out_shape = [jax.ShapeDtypeStruct((B, n_slabs, S, LANES), BF16),
                 jax.ShapeDtypeStruct((B, S, LANES), F32),
                 jax.ShapeDtypeStruct((B, 2 * GDN_HEADS, S), F32)]
    return out_specs, out_shape


def _gdn_in_scratch(tm):
    return [pltpu.VMEM((2, GDN_SLAB_GROUP, tm + CONV_HALO, LANES), F32),
            pltpu.VMEM((3 * GDN_HEADS, CONV_HALO, LANES), F32)]


def _gdn_in_scratch_bytes(tm):
    return 2 * GDN_SLAB_GROUP * (tm + CONV_HALO) * LANES * 4 + 3 * GDN_HEADS * CONV_HALO * LANES * 4


def _gdn_params(w_in, conv_w, a_log, dt_bias):
    n, D, n_in = w_in.shape
    main = n_in - 2 * GDN_HEADS
    w_all = w_in.astype(BF16)
    w_ab = jnp.zeros((n, D, LANES), F32).at[:, :, :2 * GDN_HEADS].set(w_in[:, :, main:]).astype(BF16)
    alog = jnp.zeros((n, 1, LANES), F32).at[:, 0, :GDN_HEADS].set(a_log.astype(F32))
    dtb = jnp.zeros((n, 1, LANES), F32).at[:, 0, :GDN_HEADS].set(dt_bias.astype(F32))
    convw = conv_w.astype(F32).reshape(n, GDN_CONV, main // LANES - GDN_HEADS, 1, LANES)
    return dict(w_all=w_all, main=main, w_ab=w_ab, alog=alog, dtb=dtb, convw=convw)


def _gdn_in_inputs(gp, i):
    D = gp["w_all"].shape[1]
    ins = [gp["w_all"], gp["w_ab"], gp["alog"], gp["dtb"], gp["convw"]]
    specs = [_layer_resident(gp["w_all"], i, (D, gp["main"])), _layer_resident(gp["w_ab"], i),
             _layer_resident(gp["alog"], i), _layer_resident(gp["dtb"], i), _layer_resident(gp["convw"], i)]
    weight_bytes = (D * gp["main"] + D * LANES) * 2 + gp["convw"][0].size * 4 * SUBLANES
    return ins, specs, weight_bytes


def gdn_in_proj(x, mods, layer, gains, gp, i, tm):
    B, S, D = x.shape
    n_slabs = gp["main"] // LANES
    out_specs, out_shape = _gdn_in_outputs(B, S, tm, n_slabs)
    w_ins, w_specs, weight_bytes = _gdn_in_inputs(gp, i)
    vmem = (2 * tm * D * 4 + weight_bytes + 2 * tm * n_slabs * LANES * 2 + _gdn_in_scratch_bytes(tm)
            + 6 * tm * GDN_SLAB_GROUP * LANES * 4 + (6 << 20))
    return pl.pallas_call(
        functools.partial(_gdn_in_kernel, chunk=GDN_CHUNK),
        grid=(B, S // tm),
        in_specs=[pl.BlockSpec((None, tm, D), lambda b, s: (b, s, 0)),
                  _layer_resident(gains, layer),
                  _mod_spec(D, layer, 0), _mod_spec(D, layer, 1)] + w_specs,
        out_specs=out_specs, out_shape=out_shape,
        scratch_shapes=_gdn_in_scratch(tm),
        compiler_params=_cparams(("arbitrary", "arbitrary"), vmem),
        name="gdn_in_proj",
    )(x, gains, mods, mods, *w_ins)


def _unit_lower_inverse(lmat, ri, ci):
    n = lmat.shape[-1]
    diff = ri ^ ci
    p = jnp.where(ri == ci, 1.0, 0.0) - jnp.where(diff < 2, lmat, 0.0)
    s = 2
    while s < n:
        off = jnp.where(diff >= s, jnp.where(diff < 2 * s, lmat, 0.0), 0.0)
        p = p - _bmm(p, _bmm(off, p))
        s *= 2
    return p


def _gdn_core_kernel(q_ref, k_ref, v_ref, gb_ref, gbt_ref, o_ref, state_ref, *, rows, chunk):
    H = GDN_HEADS
    nc = rows // chunk

    @pl.when(pl.program_id(1) == 0)
    def _():
        state_ref[...] = jnp.zeros_like(state_ref)

    def chunked(t):
        if nc == 1:
            return t
        return jnp.concatenate([t[:, c * chunk:(c + 1) * chunk] for c in range(nc)], axis=0)

    q = chunked(q_ref[...])
    k = chunked(k_ref[...])
    v = chunked(v_ref[...])

    gb = gb_ref[...]
    gbt = gbt_ref[...]
    gcc = chunked(jnp.stack([gb[:, h:h + 1] for h in range(H)]))
    bc = chunked(jnp.stack([gb[:, H + h:H + h + 1] for h in range(H)]))

    def rows_of(base):
        return jnp.stack([gbt[base + h:base + h + 1, c * chunk:(c + 1) * chunk]
                          for c in range(nc) for h in range(H)])

    grc, brc = rows_of(0), rows_of(H)
    g_last = gcc[:, chunk - 1:chunk, :]

    ri = lax.broadcasted_iota(jnp.int32, (1, chunk, chunk), 1)
    ci = lax.broadcasted_iota(jnp.int32, (1, chunk, chunk), 2)
    gcb = jnp.broadcast_to(gcc, (nc * H, chunk, LANES))
    decay = jnp.exp(jnp.where(ri >= ci, gcb[:, :, :chunk] - grc, -jnp.inf))
    kk_qk = jnp.einsum("bcd,bsd->bcs", jnp.concatenate([k, q], axis=1), k,
                       preferred_element_type=F32)
    lmat = jnp.where(ri > ci, kk_qk[:, :chunk] * decay * bc, 0.0)
    attn = (kk_qk[:, chunk:] * decay).astype(BF16)
    tinv = _unit_lower_inverse(lmat, ri, ci)
    u = _bmm(tinv * brc, v)
    w = _bmm(tinv * (brc * jnp.exp(grc)), k)
    wq = jnp.concatenate([w.astype(BF16), q], axis=1)
    eg = jnp.exp(gcb)
    kt = jnp.stack([k[b].astype(F32).T for b in range(nc * H)])
    kgt = (kt * jnp.exp(g_last - grc)).astype(BF16)
    egl = jnp.exp(g_last)

    state = state_ref[...]
    for c in range(nc):
        sel = slice(c * H, (c + 1) * H)
        ws_qs = _bmm(wq[sel], state)
        v_new = (u[sel] - ws_qs[:, :chunk]).astype(BF16)
        o = ws_qs[:, chunk:] * eg[sel] + _bmm(attn[sel], v_new)
        state = state * egl[sel] + _bmm(kgt[sel], v_new)
        o_ref[:, c * chunk:(c + 1) * chunk, :] = o.astype(BF16)
    state_ref[...] = state


def gdn_core(p, gb, gbt, rows):
    B, _, S, _ = p.shape
    H = GDN_HEADS

    def slabs(group):
        return pl.BlockSpec((None, H, rows, LANES), lambda b, s: (b, group, s, 0))

    return pl.pallas_call(
        functools.partial(_gdn_core_kernel, rows=rows, chunk=GDN_CHUNK),
        grid=(B, S // rows),
        in_specs=[slabs(0), slabs(1), slabs(2),
                  pl.BlockSpec((None, rows, LANES), lambda b, s: (b, s, 0)),
                  pl.BlockSpec((None, 2 * H, rows), lambda b, s: (b, 0, s))],
        out_specs=pl.BlockSpec((None, H, rows, LANES), lambda b, s: (b, 0, s, 0)),
        out_shape=jax.ShapeDtypeStruct((B, H, S, LANES), BF16),
        scratch_shapes=[pltpu.VMEM((H, GDN_DK, LANES), F32)],
        compiler_params=_cparams(("arbitrary", "arbitrary"), 32 << 20),
        name="gdn_core",
    )(p, p, p, gb, gbt)


def _alibi_slope(head):
    return 2.0 ** (-8.0 * (head + 1) / SWA_Q_HEADS)


def _swa_kernel(sink_ref, qt_ref, kp_ref, kc_ref, vtp_ref, vtc_ref, o_ref, bias_ref, *, layer_slot, n_blocks):
    blk = pl.program_id(1)
    W = SWA_WINDOW
    hd = SWA_HEAD_DIM
    KV = SWA_KV_HEADS
    G = SWA_Q_HEADS // KV
    NEG = -jnp.inf

    @pl.when(jnp.logical_and(pl.program_id(0) == 0, blk == 0))
    def _():
        key = lax.broadcasted_iota(jnp.int32, (2 * W, G * W), 0)
        col = lax.broadcasted_iota(jnp.int32, (2 * W, G * W), 1)
        dist = (col & (W - 1)) + W - key
        valid = (dist >= 0) & (dist < SWA_WINDOW)
        distf = dist.astype(F32)
        for j in range(KV):
            slope = jnp.zeros((2 * W, G * W), F32)
            for a in range(G):
                slope = jnp.where((col >= a * W) & (col < (a + 1) * W), _alibi_slope(G * j + a), slope)
            bias = jnp.where(valid, (-LOG2_E) * slope * distf, NEG)
            bias_ref[1, j] = bias
            bias_ref[0, j] = jnp.where(key >= W, bias, NEG)

    kall = jnp.concatenate([kp_ref[...], kc_ref[...]], axis=0)
    vtall = jnp.concatenate([vtp_ref[...], vtc_ref[...]], axis=1)
    zpad = jnp.zeros((hd, W), BF16)
    ks, ws, vts = [], [], []
    for i in range(n_blocks):
        for j in range(KV):
            tile, half = divmod(j, LANES // hd)
            ks.append(kall[i * W:(i + 2) * W, tile * LANES:(tile + 1) * LANES])
            cols = []
            for a in range(G):
                h = G * j + a
                qt = qt_ref[h * hd:(h + 1) * hd, i * W:(i + 1) * W]
                cols.append(jnp.concatenate([qt, zpad] if half == 0 else [zpad, qt], axis=0))
            ws.append(jnp.concatenate(cols, axis=1))
            vts.append(vtall[j * hd:(j + 1) * hd, i * W:(i + 2) * W])
    ks, ws, vts = jnp.stack(ks), jnp.stack(ws), jnp.stack(vts)

    s = jnp.einsum("jkd,jdq->jkq", ks, ws, preferred_element_type=F32)
    bias = [bias_ref[jnp.minimum(blk, 1)]] + [bias_ref[1]] * (n_blocks - 1)
    s = s + (jnp.concatenate(bias, axis=0) if n_blocks > 1 else bias[0])
    lane = lax.broadcasted_iota(jnp.int32, (1, G * W), 1)
    sinks = []
    for j in range(KV):
        row = jnp.zeros((1, G * W), F32)
        for a in range(G):
            row = jnp.where((lane >= a * W) & (lane < (a + 1) * W), sink_ref[layer_slot, G * j + a] * LOG2_E, row)
        sinks.append(row)
    sink = jnp.stack(sinks * n_blocks)
    m = jnp.maximum(jnp.max(s, axis=1, keepdims=True), sink)
    p = jnp.exp2(s - m)
    denom = jnp.sum(p, axis=1, keepdims=True) + jnp.exp2(sink - m)
    ot = jnp.einsum("jdk,jkq->jdq", vts, p.astype(BF16), preferred_element_type=F32)
    ot = ot * (1.0 / denom)
    for i in range(n_blocks):
        for t in range(SWA_Q_HEADS // 2):
            j, a0 = divmod(2 * t, G)
            oj = ot[i * KV + j]
            pair = jnp.concatenate([oj[:, a0 * W:(a0 + 1) * W], oj[:, (a0 + 1) * W:(a0 + 2) * W]], axis=0)
            o_ref[i * W:(i + 1) * W, t * LANES:(t + 1) * LANES] = pair.T.astype(BF16)


def swa_attention(qt, k_sh, vt_sh, sinks, layer_slot):
    B, QW, S = qt.shape
    KW = k_sh.shape[-1]
    W = SWA_WINDOW
    G = SWA_Q_HEADS // SWA_KV_HEADS
    nb = SWA_BLOCKS_PER_STEP if S % (SWA_BLOCKS_PER_STEP * W) == 0 else 1
    T = nb * W
    prev = lambda n: jnp.maximum(n * nb - 1, 0)
    return pl.pallas_call(
        functools.partial(_swa_kernel, layer_slot=layer_slot, n_blocks=nb),
        grid=(B, S // T),
        in_specs=[pl.BlockSpec(memory_space=pltpu.SMEM),
                  pl.BlockSpec((None, QW, T), lambda b, n: (b, 0, n)),
                  pl.BlockSpec((None, W, KW), lambda b, n: (b, prev(n), 0)),
                  pl.BlockSpec((None, T, KW), lambda b, n: (b, n, 0)),
                  pl.BlockSpec((None, KW, W), lambda b, n: (b, 0, prev(n))),
                  pl.BlockSpec((None, KW, T), lambda b, n: (b, 0, n))],
        out_specs=pl.BlockSpec((None, T, QW), lambda b, n: (b, n, 0)),
        out_shape=jax.ShapeDtypeStruct((B, S, QW), BF16),
        scratch_shapes=[pltpu.VMEM((2, SWA_KV_HEADS, 2 * W, G * W), F32)],
        compiler_params=_cparams(("arbitrary", "arbitrary"), (8 + 6 * nb) << 20),
        name="swa_attention",
    )(sinks, qt, k_sh, k_sh, vt_sh, vt_sh)


def _post_mlp_kernel(*refs, gated, tail, ff_chunk, n_split):
    n_fixed = 12 if gated else 10
    if gated:
        (x_ref, a_ref, z_ref, onorm_ref, wo_ref, w1_ref, w2_ref, gains_ref,
         gate_mix_ref, shift_mlp_ref, scale_mlp_ref, gate_mlp_ref) = refs[:n_fixed]
    else:
        (x_ref, a_ref, wo_ref, w1_ref, w2_ref, gains_ref,
         gate_mix_ref, shift_mlp_ref, scale_mlp_ref, gate_mlp_ref) = refs[:n_fixed]
    rest = refs[n_fixed:]
    sub = x_ref.shape[0] // n_split
    parts = [slice(i * sub, (i + 1) * sub) for i in range(n_split)]
    xs = [x_ref[sl, :] for sl in parts]
    if gated:
        acts = []
        for sl in parts:
            g = (_rms(a_ref[:, sl, :].astype(F32), NORM_EPS * GDN_DK) * onorm_ref[...]
                 * _silu(z_ref[:, sl, :].astype(F32))).astype(BF16)
            acts.append(jnp.concatenate([g[h] for h in range(g.shape[0])], axis=1))
    else:
        acts = [a_ref[sl, :] for sl in parts]
    ys = [jnp.dot(act, wo_ref[...], preferred_element_type=F32) for act in acts]
    xs = [_adaln_post(x, y, gains_ref[1:2, :], gate_mix_ref[...]) for x, y in zip(xs, ys)]
    hs = [_adaln_pre(x, gains_ref[2:3, :], shift_mlp_ref[...], scale_mlp_ref[...]).astype(BF16) for x in xs]
    d_ff = w1_ref.shape[1]
    accs = [None] * n_split
    for f0 in range(0, d_ff, ff_chunk):
        for i in range(n_split):
            a = jnp.dot(hs[i], w1_ref[:, f0:f0 + ff_chunk], preferred_element_type=F32)
            a = jnp.square(jnp.maximum(a, 0.0)).astype(BF16)
            part = jnp.dot(a, w2_ref[f0:f0 + ff_chunk, :], preferred_element_type=F32)
            accs[i] = part if accs[i] is None else accs[i] + part
    xs = [_adaln_post(x, acc, gains_ref[3:4, :], gate_mlp_ref[...]) for x, acc in zip(xs, accs)]

    def store_x(xo_ref):
        for sl, x in zip(parts, xs):
            xo_ref[sl, :] = x

    def q_proj(ngains_ref, nshift_ref, nscale_ref, wq_ref, q_ref):
        for sl, x in zip(parts, xs):
            hq = _adaln_pre(x, ngains_ref[0:1, :], nshift_ref[...], nscale_ref[...]).astype(BF16)
            q = jnp.dot(hq, wq_ref[...], preferred_element_type=F32) * (SWA_HEAD_DIM ** -0.5 * LOG2_E)
            q_ref[:, sl] = q.T.astype(BF16)

    if tail == "none":
        (xo_ref,) = rest
        store_x(xo_ref)
    elif tail == "q":
        (ngains_ref, nshift_ref, nscale_ref, wq_ref, xo_ref, q_ref) = rest
        store_x(xo_ref)
        q_proj(ngains_ref, nshift_ref, nscale_ref, wq_ref, q_ref)
    else:
        raise ValueError(tail)


def post_mlp(x, act, mods, layer, gains, w_o, wo_idx, w1, w2, tm, *, gate_args=None,
             tail="none", tail_args=None, n_split=1):
    B, S, D = x.shape
    gated = gate_args is not None
    row = lambda width: pl.BlockSpec((None, tm, width), lambda b, s: (b, s, 0))
    col = lambda width: pl.BlockSpec((None, width, tm), lambda b, s: (b, 0, s))
    ins, in_specs = [x], [row(D)]
    if gated:
        z_src, z_group, onorm, oidx = gate_args
        H = act.shape[1]
        ins += [act, z_src, onorm]
        in_specs += [pl.BlockSpec((None, H, tm, LANES), lambda b, s: (b, 0, s, 0)),
                     pl.BlockSpec((None, H, tm, LANES), lambda b, s: (b, z_group, s, 0)),
                     _layer_resident(onorm, oidx)]
    else:
        ins += [act]
        in_specs += [row(act.shape[-1])]
    ins += [w_o, w1, w2, gains, mods, mods, mods, mods]
    in_specs += [_layer_resident(w_o, wo_idx), _layer_resident(w1, layer), _layer_resident(w2, layer),
                 _layer_resident(gains, layer),
                 _mod_spec(D, layer, 2), _mod_spec(D, layer, 3), _mod_spec(D, layer, 4), _mod_spec(D, layer, 5)]
    out_specs = [row(D)]
    out_shape = [jax.ShapeDtypeStruct((B, S, D), F32)]
    weight_bytes = (w_o[0].size + w1[0].size + w2[0].size) * 2
    extra = 0
    if tail == "q":
        nlayer, w_q, qi = tail_args
        qw = w_q.shape[-1]
        ins += [gains, mods, mods, w_q]
        in_specs += [_layer_resident(gains, nlayer), _mod_spec(D, nlayer, 0), _mod_spec(D, nlayer, 1),
                     _layer_resident(w_q, qi)]
        out_specs += [col(qw)]
        out_shape += [jax.ShapeDtypeStruct((B, qw, S), BF16)]
        weight_bytes += w_q[0].size * 2
        extra = 2 * tm * qw * 2
    ff_chunk = 512
    vmem = (weight_bytes + extra + 4 * tm * D * 4 + 4 * tm * D * 2 + 6 * tm * D * 4
            + 3 * tm * ff_chunk * 4 + (4 << 20))
    return pl.pallas_call(
        functools.partial(_post_mlp_kernel, gated=gated, tail=tail, ff_chunk=ff_chunk, n_split=n_split),
        grid=(B, S // tm),
        in_specs=in_specs, out_specs=out_specs, out_shape=out_shape,
        compiler_params=_cparams(("arbitrary", "arbitrary"), vmem),
        name="post_mlp_" + tail,
    )(*ins)


def _attn_proj_kernel(x_ref, kvgain_ref, kvshift_ref, kvscale_ref, wkv_ref, gains_ref, shift_ref, scale_ref,
                      wq_ref, k_ref, vt_ref, qt_ref):
    x = x_ref[...]
    hk = _adaln_pre(x, kvgain_ref[...], kvshift_ref[...], kvscale_ref[...]).astype(BF16)
    kv = jnp.dot(hk, wkv_ref[...], preferred_element_type=F32)
    kvw = kv.shape[1] // 2
    k_ref[...] = kv[:, :kvw].astype(BF16)
    vt_ref[...] = kv[:, kvw:].T.astype(BF16)
    hq = _adaln_pre(x, gains_ref[0:1, :], shift_ref[...], scale_ref[...]).astype(BF16)
    q = jnp.dot(hq, wq_ref[...], preferred_element_type=F32) * (SWA_HEAD_DIM ** -0.5 * LOG2_E)
    qt_ref[...] = q.T.astype(BF16)


def attn_proj(x, kvmods, kvgain, w_kv, mods, layer, gains, w_q, qi, tm):
    B, S, D = x.shape
    kvw = w_kv.shape[1] // 2
    qw = w_q.shape[-1]
    kv_spec = lambda k: pl.BlockSpec((None, None, 1, D), lambda b, s: (0, b, 0, k))
    col = lambda width: pl.BlockSpec((None, width, tm), lambda b, s: (b, 0, s))
    vmem = (w_kv.size + w_q[0].size) * 2 + 2 * tm * D * 4 + 4 * tm * (2 * kvw + qw) * 2 + 6 * tm * D * 4 + (4 << 20)
    return pl.pallas_call(
        _attn_proj_kernel,
        grid=(B, S // tm),
        in_specs=[pl.BlockSpec((None, tm, D), lambda b, s: (b, s, 0)),
                  _resident((1, D)), kv_spec(0), kv_spec(1), _resident(w_kv.shape),
                  _layer_resident(gains, layer), _mod_spec(D, layer, 0), _mod_spec(D, layer, 1),
                  _layer_resident(w_q, qi)],
        out_specs=[pl.BlockSpec((None, tm, kvw), lambda b, s: (b, s, 0)), col(kvw), col(qw)],
        out_shape=[jax.ShapeDtypeStruct((B, S, kvw), BF16), jax.ShapeDtypeStruct((B, kvw, S), BF16),
                   jax.ShapeDtypeStruct((B, qw, S), BF16)],
        compiler_params=_cparams(("arbitrary", "arbitrary"), vmem),
        name="attn_proj",
    )(x, kvgain.reshape(1, D), kvmods, kvmods, w_kv, gains, mods, mods, w_q)


def kernel(x, c, mod_w, mod_b, norm_g, gdn_w_in, gdn_conv, gdn_a_log, gdn_dt_bias, gdn_onorm,
           gdn_w_out, kv_mod_w, kv_mod_b, kv_norm, w_kv, swa_w_q, swa_sinks, swa_w_o,
           mlp_w1, mlp_w2):
    B, S, D = x.shape
    depth = mod_w.shape[0]
    n_gdn = gdn_w_in.shape[0]
    tm = min(S, 512)
    tm_wide = min(S, 1024)
    rows = min(S, 256)
    assert S % tm == 0 and S % rows == 0 and rows % GDN_CHUNK == 0 and S % SWA_WINDOW == 0

    mods = modulation(c, mod_w, mod_b).reshape(depth, B, 1, mod_w.shape[-1])
    kvmods = modulation(c, kv_mod_w[None], kv_mod_b[None]).reshape(1, B, 1, kv_mod_w.shape[-1])

    w1 = mlp_w1.astype(BF16)
    w2 = mlp_w2.astype(BF16)
    gp = _gdn_params(gdn_w_in, gdn_conv, gdn_a_log, gdn_dt_bias)
    gdn_wo = gdn_w_out.astype(BF16)
    swa_wq = swa_w_q.astype(BF16)
    swa_wo = swa_w_o.astype(BF16)
    wkv = w_kv.astype(BF16)
    sinks = swa_sinks.astype(F32)
    gains = norm_g.astype(F32)
    onorm = gdn_onorm.astype(F32).reshape(n_gdn, 1, LANES)

    p, gb, gbt = gdn_in_proj(x, mods, 0, gains, gp, 0, tm)
    z_group = 3
    k_sh = v_sh = q = None
    for layer in range(depth):
        nxt = layer + 1
        if layer < n_gdn:
            act = gdn_core(p, gb, gbt, rows)
            w_o, wo_idx, gate_args = gdn_wo, layer, (p, z_group, onorm, layer)
        else:
            act = swa_attention(q, k_sh, v_sh, sinks, layer - n_gdn)
            w_o, wo_idx, gate_args = swa_wo, layer - n_gdn, None
        if nxt == depth:
            tail, tail_args = "none", None
        elif nxt < n_gdn:
            tail, tail_args = "none", None
        elif nxt == n_gdn:
            tail, tail_args = "none", None
        else:
            tail, tail_args = "q", (nxt, swa_wq, nxt - n_gdn)
        wide = gate_args is None or tail == "none"
        outs = post_mlp(x, act, mods, layer, gains, w_o, wo_idx, w1, w2, tm_wide if wide else tm,
                        gate_args=gate_args, tail=tail, tail_args=tail_args,
                        n_split=tm_wide // tm if wide else 1)
        x = outs[0]
        if nxt < n_gdn:
            p, gb, gbt = gdn_in_proj(x, mods, nxt, gains, gp, nxt, tm)
        elif nxt == n_gdn:
            k_sh, v_sh, q = attn_proj(x, kvmods, kv_norm, wkv, mods, nxt, gains, swa_wq, 0, tm)
        elif tail == "q":
            (q,) = outs[1:]
    return x
```

```python
import functools

import jax
import jax.numpy as jnp
from jax import lax
from jax.experimental import pallas as pl
from jax.experimental.pallas import tpu as pltpu

F32 = jnp.float32
BF16 = jnp.bfloat16

NORM_EPS = 1e-6
LOG2_E = 1.4426950408889634
LANES = 128
SUBLANES = 8
GDN_HEADS = 8
GDN_DK = 128
GDN_CONV = 4
SWA_HEAD_DIM = 64
SWA_Q_HEADS = 16
SWA_KV_HEADS = 4
SWA_WINDOW = 128
SWA_BLOCKS_PER_STEP = 8
V7X_SCOPED_VMEM_BYTES = 60000 * 1024

GDN_CHUNK = 128
GDN_SLAB_GROUP = 2
CONV_HALO = SUBLANES


def _cparams(semantics, vmem_bytes):
    return pltpu.CompilerParams(dimension_semantics=semantics,
                                vmem_limit_bytes=min(int(vmem_bytes), V7X_SCOPED_VMEM_BYTES))


def _sigmoid(x):
    return 1.0 / (1.0 + jnp.exp2(x * (-LOG2_E)))


def _silu(x):
    h = 0.5 * x
    return h + h * jnp.tanh(h)


def _rms(x, eps=NORM_EPS):
    return x * lax.rsqrt(jnp.mean(x * x, axis=-1, keepdims=True) + eps)


def _bdot(a, b):
    return jnp.dot(a.astype(BF16), b.astype(BF16), preferred_element_type=F32)


def _bmm(a, b):
    return jnp.einsum("bij,bjk->bik", a.astype(BF16), b.astype(BF16), preferred_element_type=F32)


def _resident(shape):
    nd = len(shape)
    return pl.BlockSpec(shape, lambda *_: (0,) * nd, pipeline_mode=pl.Buffered(1))


def _layer_resident(arr, layer, block=None):
    block = tuple(arr.shape[1:]) if block is None else tuple(block)
    nd = len(block)
    return pl.BlockSpec((None,) + block, lambda *_: (layer,) + (0,) * nd, pipeline_mode=pl.Buffered(1))


def _mod_kernel(c_ref, w_ref, b_ref, o_ref):
    o_ref[0] = _bdot(_silu(c_ref[...]), w_ref[0]) + b_ref[0]


def modulation(c, w, b):
    L, D, N = w.shape
    B = c.shape[0]
    tn = min(N, 1024)
    assert N % tn == 0
    return pl.pallas_call(
        _mod_kernel,
        grid=(L, N // tn),
        in_specs=[pl.BlockSpec((B, D), lambda l, n: (0, 0)),
                  pl.BlockSpec((1, D, tn), lambda l, n: (l, 0, n)),
                  pl.BlockSpec((1, 1, tn), lambda l, n: (l, 0, n))],
        out_specs=pl.BlockSpec((1, B, tn), lambda l, n: (l, 0, n)),
        out_shape=jax.ShapeDtypeStruct((L, B, N), F32),
        compiler_params=_cparams(("arbitrary", "arbitrary"), 4 * D * tn * 4 + (4 << 20)),
        name="modulation",
    )(c, w, b.reshape(L, 1, N))


def _mod_spec(D, layer, k):
    return pl.BlockSpec((None, None, 1, D), lambda b, *_: (layer, b, 0, k))


def _adaln_pre(x, gain, shift, scale):
    return _rms(x) * (gain * (1.0 + scale)) + shift


def _adaln_post(x, y, gain, gate):
    return x + _rms(y) * ((1.0 + gate) * gain)


def _gdn_gates(ab, alog_ref, dtb_ref, out_gb_ref, out_gbt_ref, chunk):
    tm = ab.shape[0]
    lane = lax.broadcasted_iota(jnp.int32, ab.shape, 1)
    pre = ab + dtb_ref[...]
    softplus = jnp.maximum(pre, 0.0) + jnp.log(1.0 + jnp.exp(-jnp.abs(pre)))
    g = -jnp.exp(alog_ref[...]) * softplus
    g = jnp.where(lane < GDN_HEADS, g, 0.0)
    beta = _sigmoid(ab)
    r = lax.broadcasted_iota(jnp.int32, (chunk, chunk), 0)
    c = lax.broadcasted_iota(jnp.int32, (chunk, chunk), 1)
    tri = (r >= c).astype(F32)
    parts = []
    for i in range(tm // chunk):
        parts.append(jnp.dot(tri, g[i * chunk:(i + 1) * chunk], preferred_element_type=F32,
                             precision=lax.Precision.HIGHEST))
    gcum = jnp.concatenate(parts, axis=0) if len(parts) > 1 else parts[0]
    gb = jnp.where(lane < GDN_HEADS, gcum, jnp.where(lane < 2 * GDN_HEADS, beta, 0.0))
    out_gb_ref[...] = gb
    out_gbt_ref[...] = gb.T[:2 * GDN_HEADS]


def _conv_epilogue(g0, slabs, convw_ref, stage_ref, buf, carry_ref, out_ref):
    group = len(slabs)
    tm = slabs[0].shape[0]
    stage_ref[buf, :, 0:CONV_HALO, :] = carry_ref[g0:g0 + group]
    for j in range(group):
        stage_ref[buf, j, CONV_HALO:CONV_HALO + tm, :] = slabs[j]
    half = None
    for t in range(GDN_CONV):
        off = CONV_HALO - (GDN_CONV - 1) + t
        term = (0.5 * convw_ref[t, g0:g0 + group]) * stage_ref[buf, :, off:off + tm, :]
        half = term if half is None else half + term
    carry_ref[g0:g0 + group] = stage_ref[buf, :, tm:tm + CONV_HALO, :]
    y = half + half * jnp.tanh(half)
    if g0 < 2 * GDN_HEADS:
        y = y * lax.rsqrt(jnp.sum(y * y, axis=-1, keepdims=True) + NORM_EPS)
    out_ref[g0:g0 + group] = y.astype(BF16)


def _gdn_group_order(n_slabs):
    group = GDN_SLAB_GROUP
    n_conv = 3 * GDN_HEADS // group
    conv_groups, z_groups = list(range(n_conv)), list(range(n_conv, n_slabs // group))
    per_z = -(-n_conv // max(len(z_groups), 1))
    order = []
    while conv_groups or z_groups:
        order += conv_groups[:per_z]
        conv_groups = conv_groups[per_z:]
        if z_groups:
            order.append(z_groups.pop(0))
    return order


def _gdn_in_body(x, first_row_block, gain, shift_ref, scale_ref, w_ref, wab_ref, alog_ref, dtb_ref,
                 convw_ref, p_ref, gb_ref, gbt_ref, stage_ref, carry_ref, chunk):
    group = GDN_SLAB_GROUP

    @pl.when(first_row_block)
    def _():
        carry_ref[...] = jnp.zeros_like(carry_ref)

    h = _adaln_pre(x, gain, shift_ref[...], scale_ref[...]).astype(BF16)

    def epilogue(gi, res):
        g0 = gi * group
        slabs = [res[:, j * LANES:(j + 1) * LANES] for j in range(group)]
        if g0 >= 3 * GDN_HEADS:
            for j in range(group):
                p_ref[g0 + j] = slabs[j].astype(BF16)
        else:
            _conv_epilogue(g0, slabs, convw_ref, stage_ref, gi % 2, carry_ref, p_ref)

    pending = None
    for gi in _gdn_group_order(w_ref.shape[1] // LANES):
        g0 = gi * group
        res = jnp.dot(h, w_ref[:, g0 * LANES:(g0 + group) * LANES], preferred_element_type=F32)
        if pending is not None:
            epilogue(*pending)
        pending = (gi, res)
    ab = jnp.dot(h, wab_ref[...], preferred_element_type=F32)
    epilogue(*pending)
    _gdn_gates(ab, alog_ref, dtb_ref, gb_ref, gbt_ref, chunk)


def _gdn_in_kernel(x_ref, gains_ref, shift_ref, scale_ref, w_ref, wab_ref, alog_ref, dtb_ref, convw_ref,
                   p_ref, gb_ref, gbt_ref, stage_ref, carry_ref, *, chunk):
    _gdn_in_body(x_ref[...], pl.program_id(1) == 0, gains_ref[0:1, :], shift_ref, scale_ref, w_ref, wab_ref,
                 alog_ref, dtb_ref, convw_ref, p_ref, gb_ref, gbt_ref, stage_ref, carry_ref, chunk)


def _gdn_in_outputs(B, S, tm, n_slabs):
    out_specs = [pl.BlockSpec((None, n_slabs, tm, LANES), lambda b, s: (b, 0, s, 0)),
                 pl.BlockSpec((None, tm, LANES), lambda b, s: (b, s, 0)),
                 pl.BlockSpec((None, 2 * GDN_HEADS, tm), lambda b, s: (b, 0, s))]
    out_shape = [jax.ShapeDtypeStruct((B, n_slabs, S, LANES), BF16),
                 jax.ShapeDtypeStruct((B, S, LANES), F32),
                 jax.ShapeDtypeStruct((B, 2 * GDN_HEADS, S), F32)]
    return out_specs, out_shape


def _gdn_in_scratch(tm):
    return [pltpu.VMEM((2, GDN_SLAB_GROUP, tm + CONV_HALO, LANES), F32),
            pltpu.VMEM((3 * GDN_HEADS, CONV_HALO, LANES), F32)]


def _gdn_in_scratch_bytes(tm):
    return 2 * GDN_SLAB_GROUP * (tm + CONV_HALO) * LANES * 4 + 3 * GDN_HEADS * CONV_HALO * LANES * 4


def _gdn_params(w_in, conv_w, a_log, dt_bias):
    n, D, n_in = w_in.shape
    main = n_in - 2 * GDN_HEADS
    w_all = w_in.astype(BF16)
    w_ab = jnp.zeros((n, D, LANES), F32).at[:, :, :2 * GDN_HEADS].set(w_in[:, :, main:]).astype(BF16)
    alog = jnp.zeros((n, 1, LANES), F32).at[:, 0, :GDN_HEADS].set(a_log.astype(F32))
    dtb = jnp.zeros((n, 1, LANES), F32).at[:, 0, :GDN_HEADS].set(dt_bias.astype(F32))
    convw = conv_w.astype(F32).reshape(n, GDN_CONV, main // LANES - GDN_HEADS, 1, LANES)
    return dict(w_all=w_all, main=main, w_ab=w_ab, alog=alog, dtb=dtb, convw=convw)


def _gdn_in_inputs(gp, i):
    D = gp["w_all"].shape[1]
    ins = [gp["w_all"], gp["w_ab"], gp["alog"], gp["dtb"], gp["convw"]]
    specs = [_layer_resident(gp["w_all"], i, (D, gp["main"])), _layer_resident(gp["w_ab"], i),
             _layer_resident(gp["alog"], i), _layer_resident(gp["dtb"], i), _layer_resident(gp["convw"], i)]
    weight_bytes = (D * gp["main"] + D * LANES) * 2 + gp["convw"][0].size * 4 * SUBLANES
    return ins, specs, weight_bytes


def gdn_in_proj(x, mods, layer, gains, gp, i, tm):
    B, S, D = x.shape
    n_slabs = gp["main"] // LANES
    out_specs, out_shape = _gdn_in_outputs(B, S, tm, n_slabs)
    w_ins, w_specs, weight_bytes = _gdn_in_inputs(gp, i)
    vmem = (2 * tm * D * 4 + weight_bytes + 2 * tm * n_slabs * LANES * 2 + _gdn_in_scratch_bytes(tm)
            + 6 * tm * GDN_SLAB_GROUP * LANES * 4 + (6 << 20))
    return pl.pallas_call(
        functools.partial(_gdn_in_kernel, chunk=GDN_CHUNK),
        grid=(B, S // tm),
        in_specs=[pl.BlockSpec((None, tm, D), lambda b, s: (b, s, 0)),
                  _layer_resident(gains, layer),
                  _mod_spec(D, layer, 0), _mod_spec(D, layer, 1)] + w_specs,
        out_specs=out_specs, out_shape=out_shape,
        scratch_shapes=_gdn_in_scratch(tm),
        compiler_params=_cparams(("arbitrary", "arbitrary"), vmem),
        name="gdn_in_proj",
    )(x, gains, mods, mods, *w_ins)


def _unit_lower_inverse(lmat, ri, ci):
    n = lmat.shape[-1]
    diff = ri ^ ci
    p = jnp.where(ri == ci, 1.0, 0.0) - jnp.where(diff < 2, lmat, 0.0)
    s = 2
    while s < n:
        off = jnp.where(diff >= s, jnp.where(diff < 2 * s, lmat, 0.0), 0.0)
        p = p - _bmm(p, _bmm(off, p))
        s *= 2
    return p


def _gdn_core_kernel(q_ref, k_ref, v_ref, gb_ref, gbt_ref, o_ref, state_ref, *, rows, chunk):
    H = GDN_HEADS
    nc = rows // chunk

    @pl.when(pl.program_id(1) == 0)
    def _():
        state_ref[...] = jnp.zeros_like(state_ref)

    def chunked(t):
        if nc == 1:
            return t
        return jnp.concatenate([t[:, c * chunk:(c + 1) * chunk] for c in range(nc)], axis=0)

    q = chunked(q_ref[...])
    k = chunked(k_ref[...])
    v = chunked(v_ref[...])

    gb = gb_ref[...]
    gbt = gbt_ref[...]
    gcc = chunked(jnp.stack([gb[:, h:h + 1] for h in range(H)]))
    bc = chunked(jnp.stack([gb[:, H + h:H + h + 1] for h in range(H)]))

    def rows_of(base):
        return jnp.stack([gbt[base + h:base + h + 1, c * chunk:(c + 1) * chunk]
                          for c in range(nc) for h in range(H)])

    grc, brc = rows_of(0), rows_of(H)
    g_last = gcc[:, chunk - 1:chunk, :]

    ri = lax.broadcasted_iota(jnp.int32, (1, chunk, chunk), 1)
    ci = lax.broadcasted_iota(jnp.int32, (1, chunk, chunk), 2)
    gcb = jnp.broadcast_to(gcc, (nc * H, chunk, LANES))
    decay = jnp.exp(jnp.where(ri >= ci, gcb[:, :, :chunk] - grc, -jnp.inf))
    kk_qk = jnp.einsum("bcd,bsd->bcs", jnp.concatenate([k, q], axis=1), k,
                       preferred_element_type=F32)
    lmat = jnp.where(ri > ci, kk_qk[:, :chunk] * decay * bc, 0.0)
    attn = (kk_qk[:, chunk:] * decay).astype(BF16)
    tinv = _unit_lower_inverse(lmat, ri, ci)
    u = _bmm(tinv * brc, v)
    w = _bmm(tinv * (brc * jnp.exp(grc)), k)
    wq = jnp.concatenate([w.astype(BF16), q], axis=1)
    eg = jnp.exp(gcb)
    kt = jnp.stack([k[b].astype(F32).T for b in range(nc * H)])
    kgt = (kt * jnp.exp(g_last - grc)).astype(BF16)
    egl = jnp.exp(g_last)

    state = state_ref[...]
    for c in range(nc):
        sel = slice(c * H, (c + 1) * H)
        ws_qs = _bmm(wq[sel], state)
        v_new = (u[sel] - ws_qs[:, :chunk]).astype(BF16)
        o = ws_qs[:, chunk:] * eg[sel] + _bmm(attn[sel], v_new)
        state = state * egl[sel] + _bmm(kgt[sel], v_new)
        o_ref[:, c * chunk:(c + 1) * chunk, :] = o.astype(BF16)
    state_ref[...] = state


def gdn_core(p, gb, gbt, rows):
    B, _, S, _ = p.shape
    H = GDN_HEADS

    def slabs(group):
        return pl.BlockSpec((None, H, rows, LANES), lambda b, s: (b, group, s, 0))

    return pl.pallas_call(
        functools.partial(_gdn_core_kernel, rows=rows, chunk=GDN_CHUNK),
        grid=(B, S // rows),
        in_specs=[slabs(0), slabs(1), slabs(2),
                  pl.BlockSpec((None, rows, LANES), lambda b, s: (b, s, 0)),
                  pl.BlockSpec((None, 2 * H, rows), lambda b, s: (b, 0, s))],
        out_specs=pl.BlockSpec((None, H, rows, LANES), lambda b, s: (b, 0, s, 0)),
        out_shape=jax.ShapeDtypeStruct((B, H, S, LANES), BF16),
        scratch_shapes=[pltpu.VMEM((H, GDN_DK, LANES), F32)],
        compiler_params=_cparams(("arbitrary", "arbitrary"), 32 << 20),
        name="gdn_core",
    )(p, p, p, gb, gbt)


def _alibi_slope(head):
    return 2.0 ** (-8.0 * (head + 1) / SWA_Q_HEADS)


def _swa_kernel(sink_ref, qt_ref, kp_ref, kc_ref, vtp_ref, vtc_ref, o_ref, bias_ref, *, layer_slot, n_blocks):
    blk = pl.program_id(1)
    W = SWA_WINDOW
    hd = SWA_HEAD_DIM
    KV = SWA_KV_HEADS
    G = SWA_Q_HEADS // KV
    NEG = -jnp.inf

    @pl.when(jnp.logical_and(pl.program_id(0) == 0, blk == 0))
    def _():
        key = lax.broadcasted_iota(jnp.int32, (2 * W, G * W), 0)
        col = lax.broadcasted_iota(jnp.int32, (2 * W, G * W), 1)
        dist = (col & (W - 1)) + W - key
        valid = (dist >= 0) & (dist < SWA_WINDOW)
        distf = dist.astype(F32)
        for j in range(KV):
            slope = jnp.zeros((2 * W, G * W), F32)
            for a in range(G):
                slope = jnp.where((col >= a * W) & (col < (a + 1) * W), _alibi_slope(G * j + a), slope)
            bias = jnp.where(valid, (-LOG2_E) * slope * distf, NEG)
            bias_ref[1, j] = bias
            bias_ref[0, j] = jnp.where(key >= W, bias, NEG)

    kall = jnp.concatenate([kp_ref[...], kc_ref[...]], axis=0)
    vtall = jnp.concatenate([vtp_ref[...], vtc_ref[...]], axis=1)
    zpad = jnp.zeros((hd, W), BF16)
    ks, ws, vts = [], [], []
    for i in range(n_blocks):
        for j in range(KV):
            tile, half = divmod(j, LANES // hd)
            ks.append(kall[i * W:(i + 2) * W, tile * LANES:(tile + 1) * LANES])
            cols = []
            for a in range(G):
                h = G * j + a
                qt = qt_ref[h * hd:(h + 1) * hd, i * W:(i + 1) * W]
                cols.append(jnp.concatenate([qt, zpad] if half == 0 else [zpad, qt], axis=0))
            ws.append(jnp.concatenate(cols, axis=1))
            vts.append(vtall[j * hd:(j + 1) * hd, i * W:(i + 2) * W])
    ks, ws, vts = jnp.stack(ks), jnp.stack(ws), jnp.stack(vts)

    s = jnp.einsum("jkd,jdq->jkq", ks, ws, preferred_element_type=F32)
    bias = [bias_ref[jnp.minimum(blk, 1)]] + [bias_ref[1]] * (n_blocks - 1)
    s = s + (jnp.concatenate(bias, axis=0) if n_blocks > 1 else bias[0])
    lane = lax.broadcasted_iota(jnp.int32, (1, G * W), 1)
    sinks = []
    for j in range(KV):
        row = jnp.zeros((1, G * W), F32)
        for a in range(G):
            row = jnp.where((lane >= a * W) & (lane < (a + 1) * W), sink_ref[layer_slot, G * j + a] * LOG2_E, row)
        sinks.append(row)
    sink = jnp.stack(sinks * n_blocks)
    m = jnp.maximum(jnp.max(s, axis=1, keepdims=True), sink)
    p = jnp.exp2(s - m)
    denom = jnp.sum(p, axis=1, keepdims=True) + jnp.exp2(sink - m)
    ot = jnp.einsum("jdk,jkq->jdq", vts, p.astype(BF16), preferred_element_type=F32)
    ot = ot * (1.0 / denom)
    for i in range(n_blocks):
        for t in range(SWA_Q_HEADS // 2):
            j, a0 = divmod(2 * t, G)
            oj = ot[i * KV + j]
            pair = jnp.concatenate([oj[:, a0 * W:(a0 + 1) * W], oj[:, (a0 + 1) * W:(a0 + 2) * W]], axis=0)
            o_ref[i * W:(i + 1) * W, t * LANES:(t + 1) * LANES] = pair.T.astype(BF16)


def swa_attention(qt, k_sh, vt_sh, sinks, layer_slot):
    B, QW, S = qt.shape
    KW = k_sh.shape[-1]
    W = SWA_WINDOW
    G = SWA_Q_HEADS // SWA_KV_HEADS
    nb = SWA_BLOCKS_PER_STEP if S % (SWA_BLOCKS_PER_STEP * W) == 0 else 1
    T = nb * W
    prev = lambda n: jnp.maximum(n * nb - 1, 0)
    return pl.pallas_call(
        functools.partial(_swa_kernel, layer_slot=layer_slot, n_blocks=nb),
        grid=(B, S // T),
        in_specs=[pl.BlockSpec(memory_space=pltpu.SMEM),
                  pl.BlockSpec((None, QW, T), lambda b, n: (b, 0, n)),
                  pl.BlockSpec((None, W, KW), lambda b, n: (b, prev(n), 0)),
                  pl.BlockSpec((None, T, KW), lambda b, n: (b, n, 0)),
                  pl.BlockSpec((None, KW, W), lambda b, n: (b, 0, prev(n))),
                  pl.BlockSpec((None, KW, T), lambda b, n: (b, 0, n))],
        out_specs=pl.BlockSpec((None, T, QW), lambda b, n: (b, n, 0)),
        out_shape=jax.ShapeDtypeStruct((B, S, QW), BF16),
        scratch_shapes=[pltpu.VMEM((2, SWA_KV_HEADS, 2 * W, G * W), F32)],
        compiler_params=_cparams(("arbitrary", "arbitrary"), (8 + 6 * nb) << 20),
        name="swa_attention",
    )(sinks, qt, k_sh, k_sh, vt_sh, vt_sh)


def _post_mlp_kernel(*refs, gated, tail, ff_chunk, n_split):
    n_fixed = 12 if gated else 10
    if gated:
        (x_ref, a_ref, z_ref, onorm_ref, wo_ref, w1_ref, w2_ref, gains_ref,
         gate_mix_ref, shift_mlp_ref, scale_mlp_ref, gate_mlp_ref) = refs[:n_fixed]
    else:
        (x_ref, a_ref, wo_ref, w1_ref, w2_ref, gains_ref,
         gate_mix_ref, shift_mlp_ref, scale_mlp_ref, gate_mlp_ref) = refs[:n_fixed]
    rest = refs[n_fixed:]
    sub = x_ref.shape[0] // n_split
    parts = [slice(i * sub, (i + 1) * sub) for i in range(n_split)]
    xs = [x_ref[sl, :] for sl in parts]
    if gated:
        acts = []
        for sl in parts:
            g = (_rms(a_ref[:, sl, :].astype(F32), NORM_EPS * GDN_DK) * onorm_ref[...]
                 * _silu(z_ref[:, sl, :].astype(F32))).astype(BF16)
            acts.append(jnp.concatenate([g[h] for h in range(g.shape[0])], axis=1))
    else:
        acts = [a_ref[sl, :] for sl in parts]
    ys = [jnp.dot(act, wo_ref[...], preferred_element_type=F32) for act in acts]
    xs = [_adaln_post(x, y, gains_ref[1:2, :], gate_mix_ref[...]) for x, y in zip(xs, ys)]
    hs = [_adaln_pre(x, gains_ref[2:3, :], shift_mlp_ref[...], scale_mlp_ref[...]).astype(BF16) for x in xs]
    d_ff = w1_ref.shape[1]
    accs = [None] * n_split
    for f0 in range(0, d_ff, ff_chunk):
        for i in range(n_split):
            a = jnp.dot(hs[i], w1_ref[:, f0:f0 + ff_chunk], preferred_element_type=F32)
            a = jnp.square(jnp.maximum(a, 0.0)).astype(BF16)
            part = jnp.dot(a, w2_ref[f0:f0 + ff_chunk, :], preferred_element_type=F32)
            accs[i] = part if accs[i] is None else accs[i] + part
    xs = [_adaln_post(x, acc, gains_ref[3:4, :], gate_mlp_ref[...]) for x, acc in zip(xs, accs)]

    def store_x(xo_ref):
        for sl, x in zip(parts, xs):
            xo_ref[sl, :] = x

    def q_proj(ngains_ref, nshift_ref, nscale_ref, wq_ref, q_ref):
        for sl, x in zip(parts, xs):
            hq = _adaln_pre(x, ngains_ref[0:1, :], nshift_ref[...], nscale_ref[...]).astype(BF16)
            q = jnp.dot(hq, wq_ref[...], preferred_element_type=F32) * (SWA_HEAD_DIM ** -0.5 * LOG2_E)
            q_ref[:, sl] = q.T.astype(BF16)

    if tail == "none":
        (xo_ref,) = rest
        store_x(xo_ref)
    elif tail == "kv_q":
        (kvgain_ref, kvshift_ref, kvscale_ref, wkv_ref, ngains_ref, nshift_ref, nscale_ref, wq_ref,
         xo_ref, k_ref, v_ref, q_ref) = rest
        store_x(xo_ref)
        for sl, x in zip(parts, xs):
            hk = _adaln_pre(x, kvgain_ref[...], kvshift_ref[...], kvscale_ref[...]).astype(BF16)
            kv = jnp.dot(hk, wkv_ref[...], preferred_element_type=F32)
            kvw = kv.shape[1] // 2
            k_ref[sl, :] = kv[:, :kvw].astype(BF16)
            v_ref[:, sl] = kv[:, kvw:].T.astype(BF16)
        q_proj(ngains_ref, nshift_ref, nscale_ref, wq_ref, q_ref)
    elif tail == "q":
        (ngains_ref, nshift_ref, nscale_ref, wq_ref, xo_ref, q_ref) = rest
        store_x(xo_ref)
        q_proj(ngains_ref, nshift_ref, nscale_ref, wq_ref, q_ref)
    else:
        raise ValueError(tail)


def post_mlp(x, act, mods, layer, gains, w_o, wo_idx, w1, w2, tm, *, gate_args=None,
             tail="none", tail_args=None, n_split=1):
    B, S, D = x.shape
    gated = gate_args is not None
    row = lambda width: pl.BlockSpec((None, tm, width), lambda b, s: (b, s, 0))
    col = lambda width: pl.BlockSpec((None, width, tm), lambda b, s: (b, 0, s))
    ins, in_specs = [x], [row(D)]
    if gated:
        z_src, z_group, onorm, oidx = gate_args
        H = act.shape[1]
        ins += [act, z_src, onorm]
        in_specs += [pl.BlockSpec((None, H, tm, LANES), lambda b, s: (b, 0, s, 0)),
                     pl.BlockSpec((None, H, tm, LANES), lambda b, s: (b, z_group, s, 0)),
                     _layer_resident(onorm, oidx)]
    else:
        ins += [act]
        in_specs += [row(act.shape[-1])]
    ins += [w_o, w1, w2, gains, mods, mods, mods, mods]
    in_specs += [_layer_resident(w_o, wo_idx), _layer_resident(w1, layer), _layer_resident(w2, layer),
                 _layer_resident(gains, layer),
                 _mod_spec(D, layer, 2), _mod_spec(D, layer, 3), _mod_spec(D, layer, 4), _mod_spec(D, layer, 5)]
    out_specs = [row(D)]
    out_shape = [jax.ShapeDtypeStruct((B, S, D), F32)]
    weight_bytes = (w_o[0].size + w1[0].size + w2[0].size) * 2
    extra = 0
    if tail == "kv_q":
        kvmods, kvgain, w_kv, nlayer, w_q, qi = tail_args
        kv_spec = lambda k: pl.BlockSpec((None, None, 1, D), lambda b, s: (0, b, 0, k))
        ins += [kvgain.reshape(1, D), kvmods, kvmods, w_kv, gains, mods, mods, w_q]
        in_specs += [_resident((1, D)), kv_spec(0), kv_spec(1), _resident(w_kv.shape),
                     _layer_resident(gains, nlayer), _mod_spec(D, nlayer, 0), _mod_spec(D, nlayer, 1),
                     _layer_resident(w_q, qi)]
        kvw = w_kv.shape[1] // 2
        qw = w_q.shape[-1]
        out_specs += [row(kvw), col(kvw), col(qw)]
        out_shape += [jax.ShapeDtypeStruct((B, S, kvw), BF16), jax.ShapeDtypeStruct((B, kvw, S), BF16),
                      jax.ShapeDtypeStruct((B, qw, S), BF16)]
        weight_bytes += (w_kv.size + w_q[0].size) * 2
        extra = 2 * tm * (2 * kvw + qw) * 2
    elif tail == "q":
        nlayer, w_q, qi = tail_args
        qw = w_q.shape[-1]
        ins += [gains, mods, mods, w_q]
        in_specs += [_layer_resident(gains, nlayer), _mod_spec(D, nlayer, 0), _mod_spec(D, nlayer, 1),
                     _layer_resident(w_q, qi)]
        out_specs += [col(qw)]
        out_shape += [jax.ShapeDtypeStruct((B, qw, S), BF16)]
        weight_bytes += w_q[0].size * 2
        extra = 2 * tm * qw * 2
    ff_chunk = 512
    vmem = (weight_bytes + extra + 4 * tm * D * 4 + 4 * tm * D * 2 + 6 * tm * D * 4
            + 3 * tm * ff_chunk * 4 + (4 << 20))
    return pl.pallas_call(
        functools.partial(_post_mlp_kernel, gated=gated, tail=tail, ff_chunk=ff_chunk, n_split=n_split),
        grid=(B, S // tm),
        in_specs=in_specs, out_specs=out_specs, out_shape=out_shape,
        compiler_params=_cparams(("arbitrary", "arbitrary"), vmem),
        name="post_mlp_" + tail,
    )(*ins)


def kernel(x, c, mod_w, mod_b, norm_g, gdn_w_in, gdn_conv, gdn_a_log, gdn_dt_bias, gdn_onorm,
           gdn_w_out, kv_mod_w, kv_mod_b, kv_norm, w_kv, swa_w_q, swa_sinks, swa_w_o,
           mlp_w1, mlp_w2):
    B, S, D = x.shape
    depth = mod_w.shape[0]
    n_gdn = gdn_w_in.shape[0]
    tm = min(S, 512)
    tm_wide = min(S, 1024)
    rows = min(S, 256)
    assert S % tm == 0 and S % rows == 0 and rows % GDN_CHUNK == 0 and S % SWA_WINDOW == 0

    mods = modulation(c, mod_w, mod_b).reshape(depth, B, 1, mod_w.shape[-1])
    kvmods = modulation(c, kv_mod_w[None], kv_mod_b[None]).reshape(1, B, 1, kv_mod_w.shape[-1])

    w1 = mlp_w1.astype(BF16)
    w2 = mlp_w2.astype(BF16)
    gp = _gdn_params(gdn_w_in, gdn_conv, gdn_a_log, gdn_dt_bias)
    gdn_wo = gdn_w_out.astype(BF16)
    swa_wq = swa_w_q.astype(BF16)
    swa_wo = swa_w_o.astype(BF16)
    wkv = w_kv.astype(BF16)
    sinks = swa_sinks.astype(F32)
    gains = norm_g.astype(F32)
    onorm = gdn_onorm.astype(F32).reshape(n_gdn, 1, LANES)

    p, gb, gbt = gdn_in_proj(x, mods, 0, gains, gp, 0, tm)
    z_group = 3
    k_sh = v_sh = q = None
    for layer in range(depth):
        nxt = layer + 1
        if layer < n_gdn:
            act = gdn_core(p, gb, gbt, rows)
            w_o, wo_idx, gate_args = gdn_wo, layer, (p, z_group, onorm, layer)
        else:
            act = swa_attention(q, k_sh, v_sh, sinks, layer - n_gdn)
            w_o, wo_idx, gate_args = swa_wo, layer - n_gdn, None
        if nxt == depth:
            tail, tail_args = "none", None
        elif nxt < n_gdn:
            tail, tail_args = "none", None
        elif nxt == n_gdn:
            tail, tail_args = "kv_q", (kvmods, kv_norm, wkv, nxt, swa_wq, 0)
        else:
            tail, tail_args = "q", (nxt, swa_wq, nxt - n_gdn)
        wide = gate_args is None or tail == "none"
        outs = post_mlp(x, act, mods, layer, gains, w_o, wo_idx, w1, w2, tm_wide if wide else tm,
                        gate_args=gate_args, tail=tail, tail_args=tail_args,
                        n_split=tm_wide // tm if wide else 1)
        x = outs[0]
        if nxt < n_gdn:
            p, gb, gbt = gdn_in_proj(x, mods, nxt, gains, gp, nxt, tm)
        elif tail == "kv_q":
            k_sh, v_sh, q = outs[1:]
        elif tail == "q":
            (q,) = outs[1:]
    return x
```
